```python
import jax, jax.numpy as jnp
from jax import lax
import numpy as np

D_MODEL = 2048
BATCH = 8
SEQ = 2048
DEPTH = 4
DEC_BATCH = 32
DEC_SEQ = 64
PAST_LEN = 2048

CHUNK = 64
N_Q_HEADS = 16
N_KV_HEADS = 4
HEAD_DIM = 64
GQA_GROUP = N_Q_HEADS // N_KV_HEADS
WINDOW = 128
WINDOW_CHUNKS = WINDOW // CHUNK
ATTN_WIDTH = N_Q_HEADS * HEAD_DIM
KV_WIDTH = N_KV_HEADS * HEAD_DIM
POOL_WINDOWS = (2, 4, 8, 16)
N_POOL_GROUPS = 4
POOL_WIDTH = 1024
POOL_GROUP_WIDTH = POOL_WIDTH // N_POOL_GROUPS
POOL_HIST = max(POOL_WINDOWS) - 1
IN_WIDTH = ATTN_WIDTH + 2 * KV_WIDTH + POOL_WIDTH + 2 * D_MODEL
SPLITS = (ATTN_WIDTH, ATTN_WIDTH + KV_WIDTH, ATTN_WIDTH + 2 * KV_WIDTH,
          ATTN_WIDTH + 2 * KV_WIDTH + POOL_WIDTH, ATTN_WIDTH + 2 * KV_WIDTH + POOL_WIDTH + D_MODEL)
N_EXPERTS = 64
TOP_K = 8
N_EXPERT_GROUPS = 8
TOPK_GROUPS = 4
EXPERT_DIM = 512
SHARED_DIM = 512
ROUTED_SCALE = 2.5
EXPERT_BLOCK = 64
LN_EPS = 1e-5
DEEPNORM_ALPHA = (2 * DEPTH) ** 0.25
DEEPNORM_BETA = (8 * DEPTH) ** -0.25

kernel_name = "streaming_swa_sink_pool_moe_deepnorm"


def layer_norm(x, g, b):
    xf = x.astype(jnp.float32)
    mu = xf.mean(-1, keepdims=True)
    var = jnp.square(xf - mu).mean(-1, keepdims=True)
    return ((xf - mu) * lax.rsqrt(var + LN_EPS) * g + b).astype(x.dtype)


def sink_attention(q, k, v, valid, sink):
    s = jnp.einsum('bnqhgd,bnkhd->bnhgqk', q, k, preferred_element_type=jnp.float32) * (HEAD_DIM ** -0.5)
    s = jnp.where(valid[None, :, None, None, None, :], s, -jnp.inf)
    sk = sink.astype(jnp.float32).reshape(1, 1, N_KV_HEADS, GQA_GROUP, 1)
    m = jnp.maximum(s.max(-1), sk)
    p = jnp.exp(s - m[..., None])
    denom = p.sum(-1) + jnp.exp(sk - m)
    p = p / denom[..., None]
    return jnp.einsum('bnhgqk,bnkhd->bnqhgd', p.astype(v.dtype), v,
                      preferred_element_type=jnp.float32).astype(v.dtype)


def banded_window_attention(q, k, v, sink):
    b, s = q.shape[:2]
    nc = s // CHUNK
    qc = q.reshape(b, nc, CHUNK, N_KV_HEADS, GQA_GROUP, HEAD_DIM)

    def band(t):
        t = t.reshape(b, nc, CHUNK, N_KV_HEADS, HEAD_DIM)
        tp = jnp.pad(t, ((0, 0), (WINDOW_CHUNKS, 0), (0, 0), (0, 0), (0, 0)))
        return jnp.concatenate([tp[:, j:j + nc] for j in range(WINDOW_CHUNKS + 1)], axis=2)

    kb, vb = band(k), band(v)
    key_chunk = (jnp.arange(nc)[:, None] - WINDOW_CHUNKS
                 + jnp.repeat(jnp.arange(WINDOW_CHUNKS + 1), CHUNK)[None, :])
    o = sink_attention(qc, kb, vb, key_chunk >= 0, sink)
    return o.reshape(b, s, ATTN_WIDTH)


def cached_window_attention(q, k, v, k_hist, v_hist, sink):
    b, l = q.shape[:2]
    kk = jnp.concatenate([k_hist, k], axis=1)
    vv = jnp.concatenate([v_hist, v], axis=1)
    qc = q.reshape(b, 1, l, N_KV_HEADS, GQA_GROUP, HEAD_DIM)
    valid = jnp.ones((1, kk.shape[1]), dtype=bool)
    o = sink_attention(qc, kk[:, None], vv[:, None], valid, sink)
    return o.reshape(b, l, ATTN_WIDTH), kk[:, -WINDOW:], vv[:, -WINDOW:]


def multiscale_pool(u, hist, pos0, w_pool, pool_scale):
    b, l, _ = u.shape
    up = jnp.concatenate([hist, u], axis=1)
    cs = jnp.pad(jnp.cumsum(up.astype(jnp.float32), axis=1), ((0, 0), (1, 0), (0, 0)))
    pos = pos0 + jnp.arange(l)
    outs = []
    for g, w in enumerate(POOL_WINDOWS):
        sl = slice(g * POOL_GROUP_WIDTH, (g + 1) * POOL_GROUP_WIDTH)
        tot = cs[:, POOL_HIST + 1:POOL_HIST + 1 + l, sl] - cs[:, POOL_HIST + 1 - w:POOL_HIST + 1 - w + l, sl]
        cnt = jnp.minimum(pos + 1, w).astype(jnp.float32)
        outs.append(tot / cnt[None, :, None])
    pooled = jnp.concatenate(outs, axis=-1)
    d = (pooled - u.astype(jnp.float32)).astype(u.dtype).reshape(b, l, N_POOL_GROUPS, POOL_GROUP_WIDTH)
    y = jnp.einsum('blgc,gcd->blgd', d, w_pool).reshape(b, l, POOL_WIDTH)
    return y * pool_scale, up[:, -POOL_HIST:]


def swiglu(x, wg, wu, wd):
    return (jax.nn.silu(x @ wg) * (x @ wu)) @ wd


def route(x, w_router, bias):
    t = x.shape[0]
    s = jax.nn.sigmoid((x @ w_router).astype(jnp.float32))
    sb = s + bias.astype(jnp.float32)
    grp = sb.reshape(t, N_EXPERT_GROUPS, N_EXPERTS // N_EXPERT_GROUPS)
    gscore = lax.top_k(grp, 2)[0].sum(-1)
    _, gidx = lax.top_k(gscore, TOPK_GROUPS)
    gmask = jax.nn.one_hot(gidx, N_EXPERT_GROUPS, dtype=jnp.float32).sum(-2) > 0
    emask = jnp.repeat(gmask, N_EXPERTS // N_EXPERT_GROUPS, axis=-1)
    _, eidx = lax.top_k(jnp.where(emask, sb, -jnp.inf), TOP_K)
    w = jnp.take_along_axis(s, eidx, axis=-1)
    w = w / w.sum(-1, keepdims=True) * ROUTED_SCALE
    return eidx, w


def routed_experts(x, eidx, gate_w, w_gate, w_up, w_down):
    t, d = x.shape
    tk = t * TOP_K
    nb = -(-tk // EXPERT_BLOCK) + N_EXPERTS
    n_slots = nb * EXPERT_BLOCK
    e_flat = eidx.reshape(-1)
    tok_flat = jnp.repeat(jnp.arange(t, dtype=jnp.int32), TOP_K)
    w_flat = gate_w.reshape(-1)
    order = jnp.argsort(e_flat)
    e_sorted = e_flat[order]
    counts = jnp.bincount(e_flat, length=N_EXPERTS)
    starts = jnp.cumsum(counts) - counts
    padded = (counts + EXPERT_BLOCK - 1) // EXPERT_BLOCK * EXPERT_BLOCK
    pend = jnp.cumsum(padded)
    pstart = pend - padded
    dest = pstart[e_sorted] + jnp.arange(tk) - starts[e_sorted]
    slot_tok = jnp.full((n_slots,), t, jnp.int32).at[dest].set(tok_flat[order])
    slot_w = jnp.zeros((n_slots,), jnp.float32).at[dest].set(w_flat[order])
    block_e = jnp.minimum(jnp.searchsorted(pend, jnp.arange(nb) * EXPERT_BLOCK, side='right'), N_EXPERTS - 1)
    x_pad = jnp.concatenate([x, jnp.zeros((1, d), x.dtype)], axis=0)

    def expert_block(args):
        toks, e = args
        return swiglu(x_pad[toks], w_gate[e], w_up[e], w_down[e])

    ys = lax.map(expert_block, (slot_tok.reshape(nb, EXPERT_BLOCK), block_e)).reshape(n_slots, d)
    out = jnp.zeros((t + 1, d), jnp.float32).at[slot_tok].add(ys.astype(jnp.float32) * slot_w[:, None])
    return out[:t].astype(x.dtype)


def trunk_layer(x, p, pos0, k_hist, v_hist, pool_hist):
    b, l, _ = x.shape
    z = x @ p['w_in']
    q, k, v, u, ga, gb = jnp.split(z, SPLITS, axis=-1)
    q = q.reshape(b, l, N_Q_HEADS, HEAD_DIM)
    k = k.reshape(b, l, N_KV_HEADS, HEAD_DIM)
    v = v.reshape(b, l, N_KV_HEADS, HEAD_DIM)
    if k_hist is None:
        a = banded_window_attention(q, k, v, p['sinks'])
        k_new, v_new = k[:, -WINDOW:], v[:, -WINDOW:]
        pool_hist = jnp.zeros((b, POOL_HIST, POOL_WIDTH), x.dtype)
    else:
        a, k_new, v_new = cached_window_attention(q, k, v, k_hist, v_hist, p['sinks'])
    pb, pool_new = multiscale_pool(u, pool_hist, pos0, p['w_pool'], p['pool_scale'])
    merged = jax.nn.sigmoid(ga) * (a @ p['w_branch_a']) + jax.nn.sigmoid(gb) * (pb @ p['w_branch_b'])
    x = layer_norm(DEEPNORM_ALPHA * x + merged @ p['w_out'], p['ln1_g'], p['ln1_b'])
    h = x.reshape(b * l, D_MODEL)
    eidx, gw = route(h, p['w_router'], p['router_bias'])
    f = swiglu(h, p['w_gate_s'], p['w_up_s'], p['w_down_s']) + routed_experts(
        h, eidx, gw, p['w_gate_e'], p['w_up_e'], p['w_down_e'])
    x = layer_norm(DEEPNORM_ALPHA * x + f.reshape(b, l, D_MODEL), p['ln2_g'], p['ln2_b'])
    return x, k_new, v_new, pool_new


def setup_inputs(seed: int = 0) -> dict:
    key = jax.random.key(seed)
    ks = jax.random.split(key, 32)
    f32 = jnp.float32

    def nrm(k, shape, scale):
        return jax.random.normal(k, shape, f32) * scale

    beta = DEEPNORM_BETA
    return {
        "x_prompt": nrm(ks[0], (BATCH, SEQ, D_MODEL), 1.0),
        "x_sample": nrm(ks[1], (DEC_BATCH, DEC_SEQ, D_MODEL), 1.0),
        "cache_k": nrm(ks[2], (DEPTH, DEC_BATCH, WINDOW, N_KV_HEADS, HEAD_DIM), 1.0),
        "cache_v": nrm(ks[3], (DEPTH, DEC_BATCH, WINDOW, N_KV_HEADS, HEAD_DIM), 1.0),
        "state_pool": nrm(ks[4], (DEPTH, DEC_BATCH, POOL_HIST, POOL_WIDTH), 1.0),
        "ln_in_g": 1.0 + nrm(ks[5], (D_MODEL,), 0.05),
        "ln_in_b": nrm(ks[6], (D_MODEL,), 0.02),
        "w_in": nrm(ks[7], (DEPTH, D_MODEL, IN_WIDTH), D_MODEL ** -0.5),
        "sinks": nrm(ks[8], (DEPTH, N_Q_HEADS), 0.5),
        "w_pool": nrm(ks[9], (DEPTH, N_POOL_GROUPS, POOL_GROUP_WIDTH, POOL_GROUP_WIDTH), POOL_GROUP_WIDTH ** -0.5),
        "pool_scale": 1.0 + nrm(ks[10], (DEPTH, POOL_WIDTH), 0.1),
        "w_branch_a": nrm(ks[11], (DEPTH, ATTN_WIDTH, D_MODEL), ATTN_WIDTH ** -0.5 * beta),
        "w_branch_b": nrm(ks[12], (DEPTH, POOL_WIDTH, D_MODEL), POOL_WIDTH ** -0.5 * beta),
        "w_out": nrm(ks[13], (DEPTH, D_MODEL, D_MODEL), D_MODEL ** -0.5 * beta),
        "ln1_g": 1.0 + nrm(ks[14], (DEPTH, D_MODEL), 0.05),
        "ln1_b": nrm(ks[15], (DEPTH, D_MODEL), 0.02),
        "w_router": nrm(ks[16], (DEPTH, D_MODEL, N_EXPERTS), D_MODEL ** -0.5),
        "router_bias": nrm(ks[17], (DEPTH, N_EXPERTS), 0.01),
        "w_gate_e": nrm(ks[18], (DEPTH, N_EXPERTS, D_MODEL, EXPERT_DIM), D_MODEL ** -0.5),
        "w_up_e": nrm(ks[19], (DEPTH, N_EXPERTS, D_MODEL, EXPERT_DIM), D_MODEL ** -0.5),
        "w_down_e": nrm(ks[20], (DEPTH, N_EXPERTS, EXPERT_DIM, D_MODEL), EXPERT_DIM ** -0.5 * beta),
        "w_gate_s": nrm(ks[21], (DEPTH, D_MODEL, SHARED_DIM), D_MODEL ** -0.5),
        "w_up_s": nrm(ks[22], (DEPTH, D_MODEL, SHARED_DIM), D_MODEL ** -0.5),
        "w_down_s": nrm(ks[23], (DEPTH, SHARED_DIM, D_MODEL), SHARED_DIM ** -0.5 * beta),
        "ln2_g": 1.0 + nrm(ks[24], (DEPTH, D_MODEL), 0.05),
        "ln2_b": nrm(ks[25], (DEPTH, D_MODEL), 0.02),
    }


def reference(x_prompt, x_sample, cache_k, cache_v, state_pool, ln_in_g, ln_in_b, w_in, sinks,
              w_pool, pool_scale, w_branch_a, w_branch_b, w_out, ln1_g, ln1_b, w_router, router_bias,
              w_gate_e, w_up_e, w_down_e, w_gate_s, w_up_s, w_down_s, ln2_g, ln2_b):
    xp = layer_norm(x_prompt, ln_in_g, ln_in_b)
    xs = layer_norm(x_sample, ln_in_g, ln_in_b)
    kp_l, vp_l, pp_l, ks_l, vs_l, ps_l = [], [], [], [], [], []
    for l in range(DEPTH):
        p = {
            'w_in': w_in[l], 'sinks': sinks[l], 'w_pool': w_pool[l], 'pool_scale': pool_scale[l],
            'w_branch_a': w_branch_a[l], 'w_branch_b': w_branch_b[l], 'w_out': w_out[l],
            'ln1_g': ln1_g[l], 'ln1_b': ln1_b[l], 'w_router': w_router[l], 'router_bias': router_bias[l],
            'w_gate_e': w_gate_e[l], 'w_up_e': w_up_e[l], 'w_down_e': w_down_e[l],
            'w_gate_s': w_gate_s[l], 'w_up_s': w_up_s[l], 'w_down_s': w_down_s[l],
            'ln2_g': ln2_g[l], 'ln2_b': ln2_b[l],
        }
        xp, kp, vp, pp = trunk_layer(xp, p, 0, None, None, None)
        xs, kn, vn, pn = trunk_layer(xs, p, PAST_LEN, cache_k[l], cache_v[l], state_pool[l])
        kp_l.append(kp); vp_l.append(vp); pp_l.append(pp)
        ks_l.append(kn); vs_l.append(vn); ps_l.append(pn)
    return (xp, xs, jnp.stack(kp_l), jnp.stack(vp_l), jnp.stack(pp_l),
            jnp.stack(ks_l), jnp.stack(vs_l), jnp.stack(ps_l))
```

```python
import dataclasses
import functools

import numpy as np
import jax
import jax.numpy as jnp
from jax import lax
from jax.experimental import pallas as pl
from jax.experimental.pallas import tpu as pltpu

F32 = jnp.float32
BF16 = jnp.bfloat16
U32 = jnp.uint32
I32 = jnp.int32

LN_EPS = 1e-5
HEAD_DIM = 64
CHUNK = 64
N_KV_HEADS = 4
GQA_GROUP = 4
WINDOW_CHUNKS = 2
POOL_WINDOWS = (2, 4, 8, 16)
POOL_HIST_ROWS = 16
N_EXPERT_GROUPS = 8
TOPK_GROUPS = 4
TOP_K = 8
ROUTED_SCALE = 2.5
ROUTER_LANES = 128
V7X_VMEM_LIMIT_BYTES = 56 * 1024 * 1024


@dataclasses.dataclass(frozen=True)
class Dims:
    d_model: int = 2048
    batch: int = 8
    seq: int = 2048
    depth: int = 4
    dec_batch: int = 32
    dec_seq: int = 64
    past_len: int = 2048
    window: int = 128
    pool_width: int = 1024
    n_experts: int = 64
    expert_dim: int = 512
    shared_dim: int = 512
    tm_ln: int = 512
    tm_proj: int = 1024
    tn_proj: int = 512
    tr_pool: int = 512
    tm_post: int = 256
    tr_route: int = 512
    tm_moe: int = 256
    bm: int = 256

    @property
    def attn_width(self):
        return N_KV_HEADS * GQA_GROUP * HEAD_DIM

    @property
    def kv_width(self):
        return N_KV_HEADS * HEAD_DIM

    @property
    def in_width(self):
        return self.attn_width + 2 * self.kv_width + self.pool_width + 2 * self.d_model

    @property
    def tp(self):
        return self.batch * self.seq

    @property
    def ts(self):
        return self.dec_batch * self.dec_seq

    @property
    def t(self):
        return self.tp + self.ts

    @property
    def n_blocks(self):
        return self.t * TOP_K // self.bm + self.n_experts

    @property
    def alpha(self):
        return (2 * self.depth) ** 0.25


def _cparams(sem):
    return pltpu.CompilerParams(dimension_semantics=sem, vmem_limit_bytes=V7X_VMEM_LIMIT_BYTES)


def _const_spec(shape):
    nd = len(shape)
    return pl.BlockSpec(shape, lambda *_: (0,) * nd, pipeline_mode=pl.Buffered(1))


def _layer_norm(h, g, b):
    mu = jnp.mean(h, axis=-1, keepdims=True)
    c = h - mu
    var = jnp.mean(c * c, axis=-1, keepdims=True)
    return c * lax.rsqrt(var + LN_EPS) * g + b


def _pack_bf16_pair(lo, hi):
    lo_bits = lax.bitcast_convert_type(lo.astype(BF16).astype(F32), U32)
    hi_bits = lax.bitcast_convert_type(hi.astype(BF16).astype(F32), U32)
    return (lo_bits >> 16) | hi_bits


def _unpack_bf16_pair(p):
    lo = lax.bitcast_convert_type(p << 16, F32)
    hi = lax.bitcast_convert_type(p & jnp.uint32(0xFFFF0000), F32)
    return lo, hi


def _ln_in_body(xp_ref, xs_ref, g_ref, b_ref, xf_ref, xb_ref, *, n_prompt_tiles):
    i = pl.program_id(0)

    def emit(x):
        y = _layer_norm(x, g_ref[...], b_ref[...])
        xf_ref[...] = y
        xb_ref[...] = y.astype(BF16)

    @pl.when(i < n_prompt_tiles)
    def _():
        emit(xp_ref[...])

    @pl.when(i >= n_prompt_tiles)
    def _():
        emit(xs_ref[...])


def _ln_in(dm, xp, xs, g, b):
    tm, d = dm.tm_ln, dm.d_model
    npt, nst = dm.tp // tm, dm.ts // tm
    return pl.pallas_call(
        functools.partial(_ln_in_body, n_prompt_tiles=npt),
        name="ln_in",
        grid=(npt + nst,),
        in_specs=[
            pl.BlockSpec((tm, d), lambda i: (jnp.minimum(i, npt - 1), 0)),
            pl.BlockSpec((tm, d), lambda i: (jnp.maximum(i - npt, 0), 0)),
            _const_spec((1, d)),
            _const_spec((1, d)),
        ],
        out_specs=[pl.BlockSpec((tm, d), lambda i: (i, 0)), pl.BlockSpec((tm, d), lambda i: (i, 0))],
        out_shape=[jax.ShapeDtypeStruct((dm.t, d), F32), jax.ShapeDtypeStruct((dm.t, d), BF16)],
        compiler_params=_cparams(("arbitrary",)),
    )(xp, xs, g, b)


def _in_proj_body(x_ref, w_ref, z_ref):
    z_ref[...] = jnp.dot(x_ref[...], w_ref[...], preferred_element_type=F32)


def _in_proj(dm, xb, w):
    tm, tn, d, n = dm.tm_proj, dm.tn_proj, dm.d_model, dm.in_width
    return pl.pallas_call(
        _in_proj_body,
        name="in_proj",
        grid=(dm.t // tm, n // tn),
        in_specs=[pl.BlockSpec((tm, d), lambda i, j: (i, 0)), pl.BlockSpec((d, tn), lambda i, j: (0, j))],
        out_specs=pl.BlockSpec((tm, tn), lambda i, j: (i, j)),
        out_shape=jax.ShapeDtypeStruct((dm.t, n), F32),
        compiler_params=_cparams(("arbitrary", "arbitrary")),
    )(xb, w)


def _attn_body(q_ref, ka_ref, kb_ref, kc_ref, va_ref, vb_ref, vc_ref, sink_ref, o_ref, *, hist_off):
    c = pl.program_id(1)
    kvw = N_KV_HEADS * HEAD_DIM
    n_heads = N_KV_HEADS * GQA_GROUP
    q = q_ref[...] * (HEAD_DIM ** -0.5)
    k = jnp.concatenate([ka_ref[...], kb_ref[...], kc_ref[...]], axis=0).astype(BF16)
    v = jnp.concatenate([va_ref[...], vb_ref[...], vc_ref[...]], axis=0).astype(BF16)
    lane_head = lax.broadcasted_iota(I32, (CHUNK, kvw), 1) >> 6
    blocks = []
    for g in range(GQA_GROUP):
        slab = q[:, g * kvw:(g + 1) * kvw]
        for h in range(N_KV_HEADS):
            blocks.append(jnp.where(lane_head == h, slab, 0.0).astype(BF16))
    lhs = jnp.concatenate(blocks, axis=0)
    s = lax.dot_general(lhs, k, (((1,), (1,)), ((), ())), preferred_element_type=F32)
    col = lax.broadcasted_iota(I32, s.shape, 1)
    n_missing = jnp.maximum(WINDOW_CHUNKS - hist_off - c, 0) * CHUNK
    s = jnp.where(col >= n_missing, s, -jnp.inf)
    sink = sink_ref[...]
    m = jnp.maximum(jnp.max(s, axis=-1, keepdims=True), sink)
    p = jnp.exp(s - m)
    denom = jnp.sum(p, axis=-1, keepdims=True) + jnp.exp(sink - m)
    p = (p / denom).astype(BF16)
    o = jnp.dot(p, v, preferred_element_type=F32)
    for g in range(GQA_GROUP):
        acc = jnp.zeros((CHUNK, kvw), F32)
        for h in range(N_KV_HEADS):
            rb = g * N_KV_HEADS + h
            acc = acc + jnp.where(lane_head == h, o[rb * CHUNK:(rb + 1) * CHUNK, :], 0.0)
        o_ref[:, g * kvw:(g + 1) * kvw] = acc.astype(BF16)
    del n_heads


def _attn_prompt(dm, z, sink_col):
    nc = dm.seq // CHUNK
    aw, kvw = dm.attn_width, dm.kv_width
    kcol = (aw + dm.pool_width + 2 * dm.d_model) // kvw
    vcol = kcol + 1

    def kv_spec(back, colblk):
        return pl.BlockSpec((CHUNK, kvw), lambda b, c: (b * nc + jnp.maximum(c - back, 0), colblk))

    return pl.pallas_call(
        functools.partial(_attn_body, hist_off=0),
        name="attn_prompt",
        grid=(dm.batch, nc),
        in_specs=[
            pl.BlockSpec((CHUNK, aw), lambda b, c: (b * nc + c, 0)),
            kv_spec(2, kcol), kv_spec(1, kcol), kv_spec(0, kcol),
            kv_spec(2, vcol), kv_spec(1, vcol), kv_spec(0, vcol),
            _const_spec((aw, 1)),
        ],
        out_specs=pl.BlockSpec((CHUNK, aw), lambda b, c: (b * nc + c, 0)),
        out_shape=jax.ShapeDtypeStruct((dm.tp, aw), BF16),
        compiler_params=_cparams(("arbitrary", "arbitrary")),
    )(z, z, z, z, z, z, z, sink_col)


def _attn_sample(dm, z, ck, cv, sink_col):
    aw, kvw = dm.attn_width, dm.kv_width
    kcol = (aw + dm.pool_width + 2 * dm.d_model) // kvw
    vcol = kcol + 1
    row0 = dm.tp // CHUNK
    assert dm.dec_seq == CHUNK and dm.window == WINDOW_CHUNKS * CHUNK

    def hist_spec(j):
        return pl.BlockSpec((CHUNK, kvw), lambda b, c: (WINDOW_CHUNKS * b + j, 0))

    def new_spec(colblk):
        return pl.BlockSpec((CHUNK, kvw), lambda b, c: (row0 + b, colblk))

    return pl.pallas_call(
        functools.partial(_attn_body, hist_off=WINDOW_CHUNKS),
        name="attn_sample",
        grid=(dm.dec_batch, 1),
        in_specs=[
            pl.BlockSpec((CHUNK, aw), lambda b, c: (row0 + b, 0)),
            hist_spec(0), hist_spec(1), new_spec(kcol),
            hist_spec(0), hist_spec(1), new_spec(vcol),
            _const_spec((aw, 1)),
        ],
        out_specs=pl.BlockSpec((CHUNK, aw), lambda b, c: (b, 0)),
        out_shape=jax.ShapeDtypeStruct((dm.ts, aw), BF16),
        compiler_params=_cparams(("arbitrary", "arbitrary")),
    )(z, ck, ck, z, cv, cv, z, sink_col)


def _pool_body(prev_ref, u_ref, o_ref, buf, *, tr, pos0, zero_first):
    i = pl.program_id(1)
    hr = POOL_HIST_ROWS
    prev = prev_ref[...]
    if zero_first:
        prev = jnp.where(i == 0, 0.0, prev)
    buf[0:hr, :] = prev
    buf[hr:hr + tr, :] = u_ref[...]
    pos = lax.broadcasted_iota(I32, (tr, 1), 0) + (pos0 + i * tr)
    gw = u_ref.shape[1] // len(POOL_WINDOWS)
    for g, w in enumerate(POOL_WINDOWS):
        cs = slice(g * gw, (g + 1) * gw)
        u = buf[hr:hr + tr, cs]
        acc = u
        for j in range(1, w):
            acc = acc + buf[hr - j:hr - j + tr, cs]
        cnt = jnp.minimum(pos + 1, w).astype(F32)
        o_ref[:, cs] = (acc / cnt - u).astype(BF16)


def _pool_prompt(dm, z):
    tr, pw, hr = dm.tr_pool, dm.pool_width, POOL_HIST_ROWS
    ucol = dm.attn_width // pw
    nt = dm.seq // tr
    return pl.pallas_call(
        functools.partial(_pool_body, tr=tr, pos0=0, zero_first=True),
        name="pool_prompt",
        grid=(dm.batch, nt),
        in_specs=[
            pl.BlockSpec((hr, pw), lambda b, i: (jnp.maximum((b * dm.seq + i * tr) // hr - 1, 0), ucol)),
            pl.BlockSpec((tr, pw), lambda b, i: (b * nt + i, ucol)),
        ],
        out_specs=pl.BlockSpec((tr, pw), lambda b, i: (b * nt + i, 0)),
        out_shape=jax.ShapeDtypeStruct((dm.tp, pw), BF16),
        scratch_shapes=[pltpu.VMEM((hr + tr, pw), F32)],
        compiler_params=_cparams(("arbitrary", "arbitrary")),
    )(z, z)


def _pool_sample(dm, z, hist):
    tr, pw, hr = dm.dec_seq, dm.pool_width, POOL_HIST_ROWS
    ucol = dm.attn_width // pw
    row0 = dm.tp // tr
    return pl.pallas_call(
        functools.partial(_pool_body, tr=tr, pos0=dm.past_len, zero_first=False),
        name="pool_sample",
        grid=(dm.dec_batch, 1),
        in_specs=[
            pl.BlockSpec((hr, pw), lambda b, i: (b, 0)),
            pl.BlockSpec((tr, pw), lambda b, i: (row0 + b, ucol)),
        ],
        out_specs=pl.BlockSpec((tr, pw), lambda b, i: (b, 0)),
        out_shape=jax.ShapeDtypeStruct((dm.ts, pw), BF16),
        scratch_shapes=[pltpu.VMEM((hr + tr, pw), F32)],
        compiler_params=_cparams(("arbitrary", "arbitrary")),
    )(hist, z)


def _post_body(ap_ref, as_ref, dp_ref, ds_ref, ga_ref, gb_ref, x_ref, wpool_ref, pscale_ref, wba_ref, wbb_ref,
               wout_ref, g_ref, b_ref, wr_ref, x1_ref, x1p_ref, lg_ref, *, alpha, n_prompt_tiles):
    is_prompt = pl.program_id(0) < n_prompt_tiles
    a = jnp.where(is_prompt, ap_ref[...], as_ref[...])
    d = jnp.where(is_prompt, dp_ref[...], ds_ref[...])
    n_groups = wpool_ref.shape[0]
    gw = d.shape[1] // n_groups
    pb = jnp.concatenate(
        [jnp.dot(d[:, g * gw:(g + 1) * gw], wpool_ref[g], preferred_element_type=F32) for g in range(n_groups)],
        axis=1)
    pb = (pb * pscale_ref[...]).astype(BF16)
    br_a = jnp.dot(a, wba_ref[...], preferred_element_type=F32)
    br_b = jnp.dot(pb, wbb_ref[...], preferred_element_type=F32)
    merged = jax.nn.sigmoid(ga_ref[...]) * br_a + jax.nn.sigmoid(gb_ref[...]) * br_b
    y = jnp.dot(merged.astype(BF16), wout_ref[...], preferred_element_type=F32)
    x1 = _layer_norm(alpha * x_ref[...] + y, g_ref[...], b_ref[...])
    x1_ref[...] = x1
    half = x1.shape[1] // 2
    x1p_ref[...] = _pack_bf16_pair(x1[:, :half], x1[:, half:])
    lg_ref[...] = jnp.dot(x1.astype(BF16), wr_ref[...], preferred_element_type=F32)


def _post(dm, a_p, a_s, d_p, d_s, z, xf, wpool, pscale, wba, wbb, wout, g1, b1, wr):
    tm, dmod, aw, pw = dm.tm_post, dm.d_model, dm.attn_width, dm.pool_width
    gcol = (aw + pw) // dmod
    assert (aw + pw) % dmod == 0
    npt = dm.tp // tm
    row = lambda i: (i, 0)
    prow = lambda i: (jnp.minimum(i, npt - 1), 0)
    srow = lambda i: (jnp.maximum(i - npt, 0), 0)
    return pl.pallas_call(
        functools.partial(_post_body, alpha=dm.alpha, n_prompt_tiles=npt),
        name="post_mixer",
        grid=(dm.t // tm,),
        in_specs=[
            pl.BlockSpec((tm, aw), prow),
            pl.BlockSpec((tm, aw), srow),
            pl.BlockSpec((tm, pw), prow),
            pl.BlockSpec((tm, pw), srow),
            pl.BlockSpec((tm, dmod), lambda i: (i, gcol)),
            pl.BlockSpec((tm, dmod), lambda i: (i, gcol + 1)),
            pl.BlockSpec((tm, dmod), row),
            _const_spec(wpool.shape), _const_spec(pscale.shape), _const_spec(wba.shape),
            _const_spec(wbb.shape), _const_spec(wout.shape), _const_spec(g1.shape), _const_spec(b1.shape),
            _const_spec(wr.shape),
        ],
        out_specs=[
            pl.BlockSpec((tm, dmod), row),
            pl.BlockSpec((tm, dmod // 2), row),
            pl.BlockSpec((tm, ROUTER_LANES), row),
        ],
        out_shape=[
            jax.ShapeDtypeStruct((dm.t, dmod), F32),
            jax.ShapeDtypeStruct((dm.t, dmod // 2), U32),
            jax.ShapeDtypeStruct((dm.t, ROUTER_LANES), F32),
        ],
        compiler_params=_cparams(("arbitrary",)),
    )(a_p, a_s, d_p, d_s, z, z, xf, wpool, pscale, wba, wbb, wout, g1, b1, wr)


def _first_index_of_max(x, iota, n):
    mx = jnp.max(x, axis=0, keepdims=True)
    return jnp.min(jnp.where(x == mx, iota, n), axis=0, keepdims=True)


def _route_body(lg_ref, bias_ref, tri_ref, eidx_ref, gw_ref, rank_ref, cnt_ref, carry, *, n_experts):
    i = pl.program_id(0)

    @pl.when(i == 0)
    def _():
        carry[...] = jnp.zeros_like(carry)

    tr = lg_ref.shape[0]
    gsz = n_experts // N_EXPERT_GROUPS
    neg = -jnp.inf
    s = jax.nn.sigmoid(lg_ref[...].T[0:n_experts, :])
    sb = s + bias_ref[...]
    io_g = lax.broadcasted_iota(I32, (gsz, tr), 0).astype(F32)
    gs_rows = []
    for gi in range(N_EXPERT_GROUPS):
        blk = sb[gi * gsz:(gi + 1) * gsz, :]
        m1 = jnp.max(blk, axis=0, keepdims=True)
        first = _first_index_of_max(blk, io_g, gsz)
        m2 = jnp.max(jnp.where(io_g == first, neg, blk), axis=0, keepdims=True)
        gs_rows.append(m1 + m2)
    gs = jnp.concatenate(gs_rows, axis=0)
    io_ng = lax.broadcasted_iota(I32, (N_EXPERT_GROUPS, tr), 0).astype(F32)
    gsel = jnp.zeros((N_EXPERT_GROUPS, tr), F32)
    cur = gs
    for _ in range(TOPK_GROUPS):
        hit = io_ng == _first_index_of_max(cur, io_ng, N_EXPERT_GROUPS)
        gsel = jnp.where(hit, 1.0, gsel)
        cur = jnp.where(hit, neg, cur)
    emask = jnp.concatenate(
        [jnp.broadcast_to(gsel[gi:gi + 1, :], (gsz, tr)) for gi in range(N_EXPERT_GROUPS)], axis=0)
    cand = jnp.where(emask > 0.0, sb, neg)
    io_e = lax.broadcasted_iota(I32, (n_experts, tr), 0).astype(F32)
    sel = jnp.zeros((n_experts, tr), F32)
    idx_rows, w_rows = [], []
    for _ in range(TOP_K):
        idx = _first_index_of_max(cand, io_e, n_experts)
        hit = io_e == idx
        w_rows.append(jnp.sum(jnp.where(hit, s, 0.0), axis=0, keepdims=True))
        idx_rows.append(idx)
        sel = jnp.where(hit, 1.0, sel)
        cand = jnp.where(hit, neg, cand)
    w = jnp.concatenate(w_rows, axis=0)
    gw_ref[...] = w / jnp.sum(w, axis=0, keepdims=True) * ROUTED_SCALE
    eidx_ref[...] = jnp.concatenate(idx_rows, axis=0).astype(I32)
    pos = carry[...] + jnp.dot(sel.astype(BF16), tri_ref[...], preferred_element_type=F32)
    rank_rows = [jnp.sum(jnp.where(io_e == idx_rows[k], pos, 0.0), axis=0, keepdims=True) for k in range(TOP_K)]
    rank_ref[...] = jnp.concatenate(rank_rows, axis=0).astype(I32)
    total = carry[...] + jnp.sum(sel, axis=1, keepdims=True)
    carry[...] = total
    cnt_ref[...] = total.astype(I32)


def _route(dm, lg, bias_col, tri):
    tr, e = dm.tr_route, dm.n_experts
    blk = lambda i: (0, i)
    return pl.pallas_call(
        functools.partial(_route_body, n_experts=e),
        name="route",
        grid=(dm.t // tr,),
        in_specs=[pl.BlockSpec((tr, ROUTER_LANES), lambda i: (i, 0)), _const_spec((e, 1)), _const_spec((tr, tr))],
        out_specs=[pl.BlockSpec((TOP_K, tr), blk), pl.BlockSpec((TOP_K, tr), blk), pl.BlockSpec((TOP_K, tr), blk),
                   pl.BlockSpec((e, 1), lambda i: (0, 0))],
        out_shape=[jax.ShapeDtypeStruct((TOP_K, dm.t), I32), jax.ShapeDtypeStruct((TOP_K, dm.t), F32),
                   jax.ShapeDtypeStruct((TOP_K, dm.t), I32), jax.ShapeDtypeStruct((e, 1), I32)],
        scratch_shapes=[pltpu.VMEM((e, 1), F32)],
        compiler_params=_cparams(("arbitrary",)),
    )(lg, bias_col, tri)


def _dispatch_body(pend_ref, padded_ref, dest_ref, x_ref, xs_ref, zbuf, zsem, sem, *, bm, n_experts):
    i = pl.program_id(0)
    tm = x_ref.shape[0]
    n_rows = xs_ref.shape[0]

    def zero_block(row_start):
        start = pl.multiple_of(row_start, bm)
        return pltpu.make_async_copy(zbuf, xs_ref.at[pl.ds(start, bm)], zsem)

    @pl.when(i == 0)
    def _():
        zbuf[...] = jnp.zeros_like(zbuf)
        used_rows = pend_ref[n_experts - 1]

        def expert_tail(e, fn):
            @pl.when(padded_ref[e] > 0)
            def _():
                fn(zero_block(pend_ref[e] - bm))

        def unused_block(b, fn):
            fn(zero_block(used_rows + b * bm))

        n_unused = (n_rows - used_rows) // bm
        for fn in (lambda c: c.start(), lambda c: c.wait()):
            lax.fori_loop(0, n_experts, lambda e, carry, fn=fn: (expert_tail(e, fn), carry)[1], 0)
            lax.fori_loop(0, n_unused, lambda b, carry, fn=fn: (unused_block(b, fn), carry)[1], 0)

    def row_copy(t, k):
        dst = dest_ref[0, 0, t * TOP_K + k]
        return pltpu.make_async_copy(x_ref.at[pl.ds(t, 1)], xs_ref.at[pl.ds(dst, 1)], sem)

    def start_tok(t, carry):
        for k in range(TOP_K):
            row_copy(t, k).start()
        return carry

    def wait_tok(t, carry):
        for k in range(TOP_K):
            row_copy(t, k).wait()
        return carry

    lax.fori_loop(0, tm, start_tok, 0)
    lax.fori_loop(0, tm, wait_tok, 0)


def _dispatch(dm, pend, padded, dest_tiles, x1p):
    tm, bm, half = dm.tm_moe, dm.bm, dm.d_model // 2
    n_rows = dm.n_blocks * bm
    return pl.pallas_call(
        functools.partial(_dispatch_body, bm=bm, n_experts=dm.n_experts),
        name="dispatch",
        grid_spec=pltpu.PrefetchScalarGridSpec(
            num_scalar_prefetch=2,
            grid=(dm.t // tm,),
            in_specs=[
                pl.BlockSpec((1, 1, tm * TOP_K), lambda i, *_: (i, 0, 0), memory_space=pltpu.SMEM),
                pl.BlockSpec((tm, half), lambda i, *_: (i, 0)),
            ],
            out_specs=pl.BlockSpec(memory_space=pl.ANY),
            scratch_shapes=[pltpu.VMEM((bm, half), U32), pltpu.SemaphoreType.DMA, pltpu.SemaphoreType.DMA],
        ),
        out_shape=jax.ShapeDtypeStruct((n_rows, half), U32),
        compiler_params=pltpu.CompilerParams(dimension_semantics=("arbitrary",), has_side_effects=True,
                                             vmem_limit_bytes=V7X_VMEM_LIMIT_BYTES),
    )(pend, padded, dest_tiles, x1p)


def _experts_body(be_ref, nu_ref, xs_ref, wg_ref, wu_ref, wd_ref, ys_ref):
    b = pl.program_id(0)

    @pl.when(b < nu_ref[0])
    def _():
        lo, hi = _unpack_bf16_pair(xs_ref[...])
        lo, hi = lo.astype(BF16), hi.astype(BF16)
        half = lo.shape[1]

        def proj(w_ref):
            return (jnp.dot(lo, w_ref[0, :half, :], preferred_element_type=F32)
                    + jnp.dot(hi, w_ref[0, half:, :], preferred_element_type=F32))

        act = jax.nn.silu(proj(wg_ref)) * proj(wu_ref)
        y = jnp.dot(act.astype(BF16), wd_ref[0], preferred_element_type=F32)
        ys_ref[...] = _pack_bf16_pair(y[:, :half], y[:, half:])

    @pl.when(b >= nu_ref[0])
    def _():
        ys_ref[...] = jnp.zeros_like(ys_ref)


def _experts(dm, block_e, n_used, xs, wg, wu, wd):
    bm, d, ed, half = dm.bm, dm.d_model, dm.expert_dim, dm.d_model // 2

    def blk(b, be, nu):
        return jnp.minimum(b, nu[0] - 1)

    return pl.pallas_call(
        _experts_body,
        name="experts",
        grid_spec=pltpu.PrefetchScalarGridSpec(
            num_scalar_prefetch=2,
            grid=(dm.n_blocks,),
            in_specs=[
                pl.BlockSpec((bm, half), lambda b, be, nu: (blk(b, be, nu), 0)),
                pl.BlockSpec((1, d, ed), lambda b, be, nu: (be[blk(b, be, nu)], 0, 0)),
                pl.BlockSpec((1, d, ed), lambda b, be, nu: (be[blk(b, be, nu)], 0, 0)),
                pl.BlockSpec((1, ed, d), lambda b, be, nu: (be[blk(b, be, nu)], 0, 0)),
            ],
            out_specs=pl.BlockSpec((bm, half), lambda b, be, nu: (b, 0)),
        ),
        out_shape=jax.ShapeDtypeStruct((dm.n_blocks * bm, half), U32),
        compiler_params=_cparams(("arbitrary",)),
    )(block_e, n_used, xs, wg, wu, wd)


def _combine_body(dest_ref, x1_ref, gw_ref, ys_ref, wgs_ref, wus_ref, wds_ref, g_ref, b_ref,
                  xf_ref, xb_ref, gbuf, sem, *, alpha):
    tm = x1_ref.shape[0]

    def row_copy(t, k):
        src = dest_ref[0, 0, t * TOP_K + k]
        return pltpu.make_async_copy(ys_ref.at[pl.ds(src, 1)], gbuf.at[k, pl.ds(t, 1)], sem)

    def start_tok(t, carry):
        for k in range(TOP_K):
            row_copy(t, k).start()
        return carry

    def wait_tok(t, carry):
        for k in range(TOP_K):
            row_copy(t, k).wait()
        return carry

    lax.fori_loop(0, tm, start_tok, 0)
    x1 = x1_ref[...]
    xb = x1.astype(BF16)
    act = (jax.nn.silu(jnp.dot(xb, wgs_ref[...], preferred_element_type=F32))
           * jnp.dot(xb, wus_ref[...], preferred_element_type=F32))
    shared = jnp.dot(act.astype(BF16), wds_ref[...], preferred_element_type=F32)
    lax.fori_loop(0, tm, wait_tok, 0)
    half = x1.shape[1] // 2
    gw = gw_ref[...]
    r_lo = jnp.zeros((tm, half), F32)
    r_hi = jnp.zeros((tm, half), F32)
    for k in range(TOP_K):
        lo, hi = _unpack_bf16_pair(gbuf[k])
        wk = gw[:, k:k + 1]
        r_lo = r_lo + lo * wk
        r_hi = r_hi + hi * wk
    f = shared + jnp.concatenate([r_lo, r_hi], axis=1)
    x2 = _layer_norm(alpha * x1 + f, g_ref[...], b_ref[...])
    xf_ref[...] = x2
    xb_ref[...] = x2.astype(BF16)


def _combine(dm, dest_tiles, x1f, gw_tok, ys, wgs, wus, wds, g2, b2):
    tm, d, half = dm.tm_moe, dm.d_model, dm.d_model // 2
    row = lambda i, *_: (i, 0)

    def const(shape):
        nd = len(shape)
        return pl.BlockSpec(shape, lambda i, *_: (0,) * nd, pipeline_mode=pl.Buffered(1))

    return pl.pallas_call(
        functools.partial(_combine_body, alpha=dm.alpha),
        name="combine",
        grid_spec=pltpu.PrefetchScalarGridSpec(
            num_scalar_prefetch=0,
            grid=(dm.t // tm,),
            in_specs=[
                pl.BlockSpec((1, 1, tm * TOP_K), lambda i: (i, 0, 0), memory_space=pltpu.SMEM),
                pl.BlockSpec((tm, d), row),
                pl.BlockSpec((tm, TOP_K), row),
                pl.BlockSpec(memory_space=pl.ANY),
                const(wgs.shape), const(wus.shape), const(wds.shape), const(g2.shape), const(b2.shape),
            ],
            out_specs=[pl.BlockSpec((tm, d), row), pl.BlockSpec((tm, d), row)],
            scratch_shapes=[pltpu.VMEM((TOP_K, tm, half), U32), pltpu.SemaphoreType.DMA],
        ),
        out_shape=[jax.ShapeDtypeStruct((dm.t, d), F32), jax.ShapeDtypeStruct((dm.t, d), BF16)],
        compiler_params=_cparams(("arbitrary",)),
    )(dest_tiles, x1f, gw_tok, ys, wgs, wus, wds, g2, b2)


def _column_order(dm):
    aw, kvw, pw, d = dm.attn_width, dm.kv_width, dm.pool_width, dm.d_model
    q = np.array([HEAD_DIM * (GQA_GROUP * h + g) + e
                  for g in range(GQA_GROUP) for h in range(N_KV_HEADS) for e in range(HEAD_DIM)])
    k0, v0, u0 = aw, aw + kvw, aw + 2 * kvw
    ga0, gb0 = u0 + pw, u0 + pw + d
    order = np.concatenate([q, np.arange(u0, u0 + pw), np.arange(ga0, ga0 + d), np.arange(gb0, gb0 + d),
                            np.arange(k0, k0 + kvw), np.arange(v0, v0 + kvw)])
    return q, order


def _moe_plan(dm, eidx, rank, counts):
    bm, e = dm.bm, dm.n_experts
    counts = counts.reshape(e)
    padded = (counts + bm - 1) // bm * bm
    pend = jnp.cumsum(padded)
    pstart = pend - padded
    onehot = eidx[None, :, :] == jnp.arange(e, dtype=I32)[:, None, None]
    dest = jnp.sum(jnp.where(onehot, pstart[:, None, None], 0), axis=0) + rank
    dest_tiles = dest.T.reshape(dm.t // dm.tm_moe, 1, dm.tm_moe * TOP_K)
    block_start = jnp.arange(dm.n_blocks, dtype=I32) * bm
    block_e = jnp.minimum(jnp.sum(block_start[:, None] >= pend[None, :], axis=1), e - 1).astype(I32)
    n_used = (pend[-1] // bm).astype(I32).reshape(1)
    return pend.astype(I32), padded.astype(I32), dest_tiles.astype(I32), block_e, n_used


def _forward(dm, x_prompt, x_sample, cache_k, cache_v, state_pool, ln_in_g, ln_in_b, w_in, sinks, w_pool,
             pool_scale, w_branch_a, w_branch_b, w_out, ln1_g, ln1_b, w_router, router_bias, w_gate_e, w_up_e,
             w_down_e, w_gate_s, w_up_s, w_down_s, ln2_g, ln2_b):
    d, aw, kvw, pw, e = dm.d_model, dm.attn_width, dm.kv_width, dm.pool_width, dm.n_experts
    qperm, order = _column_order(dm)
    row2 = lambda v: v.reshape(1, -1)
    tri = (jnp.arange(dm.tr_route)[:, None] < jnp.arange(dm.tr_route)[None, :]).astype(BF16)
    sink_heads = np.array([GQA_GROUP * h + g for g in range(GQA_GROUP) for h in range(N_KV_HEADS)])
    kcol0 = aw + pw + 2 * d

    xf, xb = _ln_in(dm, x_prompt.reshape(dm.tp, d), x_sample.reshape(dm.ts, d), row2(ln_in_g), row2(ln_in_b))
    outs = [[] for _ in range(6)]
    for l in range(dm.depth):
        z = _in_proj(dm, xb, w_in[l][:, order].astype(BF16))
        sink_col = jnp.repeat(sinks[l][sink_heads], CHUNK).reshape(aw, 1)
        a_p = _attn_prompt(dm, z, sink_col)
        a_s = _attn_sample(dm, z, cache_k[l].reshape(dm.dec_batch * dm.window, kvw),
                           cache_v[l].reshape(dm.dec_batch * dm.window, kvw), sink_col)
        hist = jnp.pad(state_pool[l], ((0, 0), (1, 0), (0, 0))).reshape(dm.dec_batch * POOL_HIST_ROWS, pw)
        d_p = _pool_prompt(dm, z)
        d_s = _pool_sample(dm, z, hist)
        wr = jnp.pad(w_router[l], ((0, 0), (0, ROUTER_LANES - e))).astype(BF16)
        x1f, x1p, lg = _post(dm, a_p, a_s, d_p, d_s, z, xf, w_pool[l].astype(BF16), row2(pool_scale[l]),
                              w_branch_a[l][qperm, :].astype(BF16), w_branch_b[l].astype(BF16),
                              w_out[l].astype(BF16), row2(ln1_g[l]), row2(ln1_b[l]), wr)
        eidx, gwt, rank, counts = _route(dm, lg, router_bias[l].reshape(e, 1), tri)
        pend, padded, dest_tiles, block_e, n_used = _moe_plan(dm, eidx, rank, counts)
        xs = _dispatch(dm, pend, padded, dest_tiles, x1p)
        ys = _experts(dm, block_e, n_used, xs, w_gate_e[l].astype(BF16), w_up_e[l].astype(BF16),
                      w_down_e[l].astype(BF16))
        xf, xb = _combine(dm, dest_tiles, x1f, gwt.T, ys, w_gate_s[l].astype(BF16), w_up_s[l].astype(BF16),
                          w_down_s[l].astype(BF16), row2(ln2_g[l]), row2(ln2_b[l]))
        zp = z[:dm.tp].reshape(dm.batch, dm.seq, dm.in_width)[:, dm.seq - dm.window:, :]
        zs = z[dm.tp:].reshape(dm.dec_batch, dm.dec_seq, dm.in_width)
        kv_shape = (N_KV_HEADS, HEAD_DIM)
        outs[0].append(zp[:, :, kcol0:kcol0 + kvw].reshape(dm.batch, dm.window, *kv_shape))
        outs[1].append(zp[:, :, kcol0 + kvw:kcol0 + 2 * kvw].reshape(dm.batch, dm.window, *kv_shape))
        outs[2].append(zp[:, dm.window - 15:, aw:aw + pw])
        ks = zs[:, :, kcol0:kcol0 + kvw].reshape(dm.dec_batch, dm.dec_seq, *kv_shape)
        vs = zs[:, :, kcol0 + kvw:kcol0 + 2 * kvw].reshape(dm.dec_batch, dm.dec_seq, *kv_shape)
        outs[3].append(jnp.concatenate([cache_k[l], ks], axis=1)[:, -dm.window:])
        outs[4].append(jnp.concatenate([cache_v[l], vs], axis=1)[:, -dm.window:])
        outs[5].append(jnp.concatenate([state_pool[l], zs[:, :, aw:aw + pw]], axis=1)[:, -15:])
    y_prompt = xf[:dm.tp].reshape(dm.batch, dm.seq, d)
    y_sample = xf[dm.tp:].reshape(dm.dec_batch, dm.dec_seq, d)
    return (y_prompt, y_sample) + tuple(jnp.stack(o) for o in outs)


def kernel(x_prompt, x_sample, cache_k, cache_v, state_pool, ln_in_g, ln_in_b, w_in, sinks, w_pool, pool_scale,
           w_branch_a, w_branch_b, w_out, ln1_g, ln1_b, w_router, router_bias, w_gate_e, w_up_e, w_down_e,
           w_gate_s, w_up_s, w_down_s, ln2_g, ln2_b):
    dm = Dims()
    return _forward(dm, x_prompt, x_sample, cache_k, cache_v, state_pool, ln_in_g, ln_in_b, w_in, sinks, w_pool,
                    pool_scale, w_branch_a, w_branch_b, w_out, ln1_g, ln1_b, w_router, router_bias, w_gate_e,
                    w_up_e, w_down_e, w_gate_s, w_up_s, w_down_s, ln2_g, ln2_b)
```

```python
import dataclasses
import functools

import numpy as np
import jax
import jax.numpy as jnp
from jax import lax
from jax.experimental import pallas as pl
from jax.experimental.pallas import tpu as pltpu

F32 = jnp.float32
BF16 = jnp.bfloat16
U32 = jnp.uint32
I32 = jnp.int32

LN_EPS = 1e-5
HEAD_DIM = 64
CHUNK = 64
N_KV_HEADS = 4
GQA_GROUP = 4
WINDOW_CHUNKS = 2
POOL_WINDOWS = (2, 4, 8, 16)
POOL_HIST_ROWS = 16
N_EXPERT_GROUPS = 8
TOPK_GROUPS = 4
TOP_K = 8
ROUTED_SCALE = 2.5
LANES = 128
ROUTER_LANES = LANES
ATTN_KEY_PAD = 2 * LANES
V7X_VMEM_LIMIT_BYTES = 56 * 1024 * 1024


@dataclasses.dataclass(frozen=True)
class Dims:
    d_model: int = 2048
    batch: int = 8
    seq: int = 2048
    depth: int = 4
    dec_batch: int = 32
    dec_seq: int = 64
    past_len: int = 2048
    window: int = 128
    pool_width: int = 1024
    n_experts: int = 64
    expert_dim: int = 512
    shared_dim: int = 512
    tm_ln: int = 512
    tm_proj: int = 1024
    tn_proj: int = 512
    attn_chunks: int = 4
    tr_pool: int = 512
    tm_post: int = 256
    tr_route: int = 512
    tm_moe: int = 256
    bm: int = 256

    @property
    def attn_width(self):
        return N_KV_HEADS * GQA_GROUP * HEAD_DIM

    @property
    def kv_width(self):
        return N_KV_HEADS * HEAD_DIM

    @property
    def in_width(self):
        return self.attn_width + 2 * self.kv_width + self.pool_width + 2 * self.d_model

    @property
    def tp(self):
        return self.batch * self.seq

    @property
    def ts(self):
        return self.dec_batch * self.dec_seq

    @property
    def t(self):
        return self.tp + self.ts

    @property
    def n_blocks(self):
        return self.t * TOP_K // self.bm + self.n_experts

    @property
    def alpha(self):
        return (2 * self.depth) ** 0.25


def _cparams(sem):
    return pltpu.CompilerParams(dimension_semantics=sem, vmem_limit_bytes=V7X_VMEM_LIMIT_BYTES)


def _const_spec(shape):
    nd = len(shape)
    return pl.BlockSpec(shape, lambda *_: (0,) * nd, pipeline_mode=pl.Buffered(1))


def _layer_norm(h, g, b):
    mu = jnp.mean(h, axis=-1, keepdims=True)
    c = h - mu
    var = jnp.mean(c * c, axis=-1, keepdims=True)
    return c * lax.rsqrt(var + LN_EPS) * g + b


def _pack_bf16_pair(lo, hi):
    lo_bits = lax.bitcast_convert_type(lo.astype(BF16).astype(F32), U32)
    hi_bits = lax.bitcast_convert_type(hi.astype(BF16).astype(F32), U32)
    return (lo_bits >> 16) | hi_bits


def _unpack_bf16_pair(p):
    lo = lax.bitcast_convert_type(p << 16, F32)
    hi = lax.bitcast_convert_type(p & jnp.uint32(0xFFFF0000), F32)
    return lo, hi


def _ln_in_body(xp_ref, xs_ref, g_ref, b_ref, xf_ref, xb_ref, *, n_prompt_tiles):
    i = pl.program_id(0)

    def emit(x):
        y = _layer_norm(x, g_ref[...], b_ref[...])
        xf_ref[...] = y
        xb_ref[...] = y.astype(BF16)

    @pl.when(i < n_prompt_tiles)
    def _():
        emit(xp_ref[...])

    @pl.when(i >= n_prompt_tiles)
    def _():
        emit(xs_ref[...])


def _ln_in(dm, xp, xs, g, b):
    tm, d = dm.tm_ln, dm.d_model
    npt, nst = dm.tp // tm, dm.ts // tm
    return pl.pallas_call(
        functools.partial(_ln_in_body, n_prompt_tiles=npt),
        name="ln_in",
        grid=(npt + nst,),
        in_specs=[
            pl.BlockSpec((tm, d), lambda i: (jnp.minimum(i, npt - 1), 0)),
            pl.BlockSpec((tm, d), lambda i: (jnp.maximum(i - npt, 0), 0)),
            _const_spec((1, d)),
            _const_spec((1, d)),
        ],
        out_specs=[pl.BlockSpec((tm, d), lambda i: (i, 0)), pl.BlockSpec((tm, d), lambda i: (i, 0))],
        out_shape=[jax.ShapeDtypeStruct((dm.t, d), F32), jax.ShapeDtypeStruct((dm.t, d), BF16)],
        compiler_params=_cparams(("arbitrary",)),
    )(xp, xs, g, b)


def _in_proj_body(x_ref, w_ref, z_ref):
    z_ref[...] = jnp.dot(x_ref[...], w_ref[...], preferred_element_type=F32)


def _in_proj(dm, xb, w):
    tm, tn, d, n = dm.tm_proj, dm.tn_proj, dm.d_model, dm.in_width
    return pl.pallas_call(
        _in_proj_body,
        name="in_proj",
        grid=(dm.t // tm, n // tn),
        in_specs=[pl.BlockSpec((tm, d), lambda i, j: (i, 0)), pl.BlockSpec((d, tn), lambda i, j: (0, j))],
        out_specs=pl.BlockSpec((tm, tn), lambda i, j: (i, j)),
        out_shape=jax.ShapeDtypeStruct((dm.t, n), F32),
        compiler_params=_cparams(("arbitrary", "arbitrary")),
    )(xb, w)


def _attn_chunk(q, k, v_ext, sink, n_missing):
    kvw = N_KV_HEADS * HEAD_DIM
    n_keys = (WINDOW_CHUNKS + 1) * CHUNK
    lane_head = lax.broadcasted_iota(I32, (CHUNK, kvw), 1) >> 6
    blocks = []
    for g in range(GQA_GROUP):
        slab = q[:, g * kvw:(g + 1) * kvw]
        for h in range(N_KV_HEADS):
            blocks.append(jnp.where(lane_head == h, slab, 0.0).astype(BF16))
    lhs = jnp.concatenate(blocks, axis=0)
    s = lax.dot_general(lhs, k, (((1,), (1,)), ((), ())), preferred_element_type=F32)
    col = lax.broadcasted_iota(I32, s.shape, 1)
    valid = col < n_keys if n_missing is None else jnp.logical_and(col < n_keys, col >= n_missing)
    s = jnp.where(valid, s, -jnp.inf)
    lanes = sink.shape[1]
    s_lo, s_hi = s[:, :lanes], s[:, lanes:]
    row_max = jnp.max(jnp.maximum(s_lo, s_hi), axis=-1, keepdims=True)
    m = jnp.maximum(jnp.broadcast_to(row_max, sink.shape), sink)
    p = jnp.concatenate([jnp.exp(s_lo - m), jnp.exp(s_hi - m)], axis=1).astype(BF16)
    o_ext = jnp.dot(p, v_ext, preferred_element_type=F32)
    denom = o_ext[:, kvw:] + jnp.exp(sink - m)
    inv = 1.0 / denom
    o = o_ext[:, :kvw] * jnp.concatenate([inv] * (kvw // lanes), axis=1)
    outs = []
    for g in range(GQA_GROUP):
        acc = jnp.zeros((CHUNK, kvw), F32)
        for h in range(N_KV_HEADS):
            rb = g * N_KV_HEADS + h
            acc = acc + jnp.where(lane_head == h, o[rb * CHUNK:(rb + 1) * CHUNK, :], 0.0)
        outs.append(acc.astype(BF16))
    return outs


def _attn_body(q_ref, kh_ref, kc_ref, vh_ref, vc_ref, sink_ref, o_ref, *, nch, banded):
    kvw = N_KV_HEADS * HEAD_DIM
    hist = WINDOW_CHUNKS * CHUNK
    pad = ATTN_KEY_PAD - hist - CHUNK
    sink = sink_ref[...]

    def with_ones(v):
        return jnp.concatenate([v, jnp.ones((v.shape[0], sink.shape[1]), BF16)], axis=1)

    kc = kc_ref[...].astype(BF16)
    vc = with_ones(vc_ref[...].astype(BF16))
    kh = kh_ref[...].astype(BF16)
    vh = with_ones(vh_ref[...].astype(BF16))
    kpad = jnp.zeros((pad, kvw), BF16)
    vpad = jnp.zeros((pad, vc.shape[1]), BF16)
    if banded:
        c0 = pl.program_id(1) * nch
        kall = jnp.concatenate([kh, kc, kpad], axis=0)
        vall = jnp.concatenate([vh, vc, vpad], axis=0)
    for j in range(nch):
        q = q_ref[j * CHUNK:(j + 1) * CHUNK, :] * (HEAD_DIM ** -0.5)
        if banded:
            k = kall[j * CHUNK:j * CHUNK + ATTN_KEY_PAD, :]
            v = vall[j * CHUNK:j * CHUNK + ATTN_KEY_PAD, :]
            n_missing = jnp.maximum(WINDOW_CHUNKS - c0 - j, 0) * CHUNK if j < WINDOW_CHUNKS else None
        else:
            k = jnp.concatenate([kh[j * hist:(j + 1) * hist, :], kc[j * CHUNK:(j + 1) * CHUNK, :], kpad], axis=0)
            v = jnp.concatenate([vh[j * hist:(j + 1) * hist, :], vc[j * CHUNK:(j + 1) * CHUNK, :], vpad], axis=0)
            n_missing = None
        outs = _attn_chunk(q, k, v, sink, n_missing)
        for g in range(GQA_GROUP):
            o_ref[j * CHUNK:(j + 1) * CHUNK, g * kvw:(g + 1) * kvw] = outs[g]


def _attn_prompt(dm, z, sink_col):
    nch = dm.attn_chunks
    rows, hist = nch * CHUNK, WINDOW_CHUNKS * CHUNK
    ns = dm.seq // rows
    aw, kvw = dm.attn_width, dm.kv_width
    kcol = (aw + dm.pool_width + 2 * dm.d_model) // kvw
    vcol = kcol + 1
    assert rows % hist == 0 and dm.seq % rows == 0

    def hist_spec(colblk):
        return pl.BlockSpec((hist, kvw), lambda b, i: (jnp.maximum((b * dm.seq + i * rows) // hist - 1, 0), colblk))

    def cur_spec(colblk):
        return pl.BlockSpec((rows, kvw), lambda b, i: (b * ns + i, colblk))

    return pl.pallas_call(
        functools.partial(_attn_body, nch=nch, banded=True),
        name="attn_prompt",
        grid=(dm.batch, ns),
        in_specs=[
            pl.BlockSpec((rows, aw), lambda b, i: (b * ns + i, 0)),
            hist_spec(kcol), cur_spec(kcol), hist_spec(vcol), cur_spec(vcol),
            _const_spec((aw, LANES)),
        ],
        out_specs=pl.BlockSpec((rows, aw), lambda b, i: (b * ns + i, 0)),
        out_shape=jax.ShapeDtypeStruct((dm.tp, aw), BF16),
        compiler_params=_cparams(("arbitrary", "arbitrary")),
    )(z, z, z, z, z, sink_col)


def _attn_sample(dm, z, ck, cv, sink_col):
    nch = dm.attn_chunks
    rows, hist = nch * CHUNK, WINDOW_CHUNKS * CHUNK
    aw, kvw = dm.attn_width, dm.kv_width
    kcol = (aw + dm.pool_width + 2 * dm.d_model) // kvw
    vcol = kcol + 1
    row0 = dm.tp // rows
    assert dm.dec_seq == CHUNK and dm.window == hist and dm.dec_batch % nch == 0 and dm.tp % rows == 0
    hist_spec = pl.BlockSpec((nch * hist, kvw), lambda b: (b, 0))

    def new_spec(colblk):
        return pl.BlockSpec((rows, kvw), lambda b: (row0 + b, colblk))

    return pl.pallas_call(
        functools.partial(_attn_body, nch=nch, banded=False),
        name="attn_sample",
        grid=(dm.dec_batch // nch,),
        in_specs=[
            pl.BlockSpec((rows, aw), lambda b: (row0 + b, 0)),
            hist_spec, new_spec(kcol), hist_spec, new_spec(vcol),
            _const_spec((aw, LANES)),
        ],
        out_specs=pl.BlockSpec((rows, aw), lambda b: (b, 0)),
        out_shape=jax.ShapeDtypeStruct((dm.ts, aw), BF16),
        compiler_params=_cparams(("arbitrary",)),
    )(z, ck, z, cv, z, sink_col)


def _pool_body(prev_ref, u_ref, o_ref, buf, *, tr, pos0, zero_first):
    i = pl.program_id(1)
    hr = POOL_HIST_ROWS
    prev = prev_ref[...]
    if zero_first:
        prev = jnp.where(i == 0, 0.0, prev)
    buf[0:hr, :] = prev
    buf[hr:hr + tr, :] = u_ref[...]
    pos = lax.broadcasted_iota(I32, (tr, 1), 0) + (pos0 + i * tr)
    gw = u_ref.shape[1] // len(POOL_WINDOWS)
    for g, w in enumerate(POOL_WINDOWS):
        cs = slice(g * gw, (g + 1) * gw)
        u = buf[hr:hr + tr, cs]
        acc = u
        for j in range(1, w):
            acc = acc + buf[hr - j:hr - j + tr, cs]
        cnt = jnp.minimum(pos + 1, w).astype(F32)
        o_ref[:, cs] = (acc / cnt - u).astype(BF16)


def _pool_prompt(dm, z):
    tr, pw, hr = dm.tr_pool, dm.pool_width, POOL_HIST_ROWS
    ucol = dm.attn_width // pw
    nt = dm.seq // tr
    return pl.pallas_call(
        functools.partial(_pool_body, tr=tr, pos0=0, zero_first=True),
        name="pool_prompt",
        grid=(dm.batch, nt),
        in_specs=[
            pl.BlockSpec((hr, pw), lambda b, i: (jnp.maximum((b * dm.seq + i * tr) // hr - 1, 0), ucol)),
            pl.BlockSpec((tr, pw), lambda b, i: (b * nt + i, ucol)),
        ],
        out_specs=pl.BlockSpec((tr, pw), lambda b, i: (b * nt + i, 0)),
        out_shape=jax.ShapeDtypeStruct((dm.tp, pw), BF16),
        scratch_shapes=[pltpu.VMEM((hr + tr, pw), F32)],
        compiler_params=_cparams(("arbitrary", "arbitrary")),
    )(z, z)


def _pool_sample(dm, z, hist):
    tr, pw, hr = dm.dec_seq, dm.pool_width, POOL_HIST_ROWS
    ucol = dm.attn_width // pw
    row0 = dm.tp // tr
    return pl.pallas_call(
        functools.partial(_pool_body, tr=tr, pos0=dm.past_len, zero_first=False),
        name="pool_sample",
        grid=(dm.dec_batch, 1),
        in_specs=[
            pl.BlockSpec((hr, pw), lambda b, i: (b, 0)),
            pl.BlockSpec((tr, pw), lambda b, i: (row0 + b, ucol)),
        ],
        out_specs=pl.BlockSpec((tr, pw), lambda b, i: (b, 0)),
        out_shape=jax.ShapeDtypeStruct((dm.ts, pw), BF16),
        scratch_shapes=[pltpu.VMEM((hr + tr, pw), F32)],
        compiler_params=_cparams(("arbitrary", "arbitrary")),
    )(hist, z)


def _post_body(ap_ref, as_ref, dp_ref, ds_ref, ga_ref, gb_ref, x_ref, wpool_ref, pscale_ref, wba_ref, wbb_ref,
               wout_ref, g_ref, b_ref, wr_ref, x1_ref, x1p_ref, lg_ref, *, alpha, n_prompt_tiles):
    is_prompt = pl.program_id(0) < n_prompt_tiles
    a = jnp.where(is_prompt, ap_ref[...], as_ref[...])
    d = jnp.where(is_prompt, dp_ref[...], ds_ref[...])
    n_groups = wpool_ref.shape[0]
    gw = d.shape[1] // n_groups
    pb = jnp.concatenate(
        [jnp.dot(d[:, g * gw:(g + 1) * gw], wpool_ref[g], preferred_element_type=F32) for g in range(n_groups)],
        axis=1)
    pb = (pb * pscale_ref[...]).astype(BF16)
    br_a = jnp.dot(a, wba_ref[...], preferred_element_type=F32)
    br_b = jnp.dot(pb, wbb_ref[...], preferred_element_type=F32)
    merged = jax.nn.sigmoid(ga_ref[...]) * br_a + jax.nn.sigmoid(gb_ref[...]) * br_b
    y = jnp.dot(merged.astype(BF16), wout_ref[...], preferred_element_type=F32)
    x1 = _layer_norm(alpha * x_ref[...] + y, g_ref[...], b_ref[...])
    x1_ref[...] = x1
    half = x1.shape[1] // 2
    x1p_ref[...] = _pack_bf16_pair(x1[:, :half], x1[:, half:])
    lg_ref[...] = jnp.dot(x1.astype(BF16), wr_ref[...], preferred_element_type=F32)


def _post(dm, a_p, a_s, d_p, d_s, z, xf, wpool, pscale, wba, wbb, wout, g1, b1, wr):
    tm, dmod, aw, pw = dm.tm_post, dm.d_model, dm.attn_width, dm.pool_width
    gcol = (aw + pw) // dmod
    assert (aw + pw) % dmod == 0
    npt = dm.tp // tm
    row = lambda i: (i, 0)
    prow = lambda i: (jnp.minimum(i, npt - 1), 0)
    srow = lambda i: (jnp.maximum(i - npt, 0), 0)
    return pl.pallas_call(
        functools.partial(_post_body, alpha=dm.alpha, n_prompt_tiles=npt),
        name="post_mixer",
        grid=(dm.t // tm,),
        in_specs=[
            pl.BlockSpec((tm, aw), prow),
            pl.BlockSpec((tm, aw), srow),
            pl.BlockSpec((tm, pw), prow),
            pl.BlockSpec((tm, pw), srow),
            pl.BlockSpec((tm, dmod), lambda i: (i, gcol)),
            pl.BlockSpec((tm, dmod), lambda i: (i, gcol + 1)),
            pl.BlockSpec((tm, dmod), row),
            _const_spec(wpool.shape), _const_spec(pscale.shape), _const_spec(wba.shape),
            _const_spec(wbb.shape), _const_spec(wout.shape), _const_spec(g1.shape), _const_spec(b1.shape),
            _const_spec(wr.shape),
        ],
        out_specs=[
            pl.BlockSpec((tm, dmod), row),
            pl.BlockSpec((tm, dmod // 2), row),
            pl.BlockSpec((tm, ROUTER_LANES), row),
        ],
        out_shape=[
            jax.ShapeDtypeStruct((dm.t, dmod), F32),
            jax.ShapeDtypeStruct((dm.t, dmod // 2), U32),
            jax.ShapeDtypeStruct((dm.t, ROUTER_LANES), F32),
        ],
        compiler_params=_cparams(("arbitrary",)),
    )(a_p, a_s, d_p, d_s, z, z, xf, wpool, pscale, wba, wbb, wout, g1, b1, wr)


def _first_index_of_max(x, iota, n):
    mx = jnp.max(x, axis=0, keepdims=True)
    return jnp.min(jnp.where(x == mx, iota, n), axis=0, keepdims=True)


def _route_body(lg_ref, bias_ref, tri_ref, eidx_ref, gw_ref, rank_ref, cnt_ref, carry, *, n_experts):
    i = pl.program_id(0)

    @pl.when(i == 0)
    def _():
        carry[...] = jnp.zeros_like(carry)

    tr = lg_ref.shape[0]
    gsz = n_experts // N_EXPERT_GROUPS
    neg = -jnp.inf
    s = jax.nn.sigmoid(lg_ref[...].T[0:n_experts, :])
    sb = s + bias_ref[...]
    io_g = lax.broadcasted_iota(I32, (gsz, tr), 0).astype(F32)
    gs_rows = []
    for gi in range(N_EXPERT_GROUPS):
        blk = sb[gi * gsz:(gi + 1) * gsz, :]
        m1 = jnp.max(blk, axis=0, keepdims=True)
        first = _first_index_of_max(blk, io_g, gsz)
        m2 = jnp.max(jnp.where(io_g == first, neg, blk), axis=0, keepdims=True)
        gs_rows.append(m1 + m2)
    gs = jnp.concatenate(gs_rows, axis=0)
    io_ng = lax.broadcasted_iota(I32, (N_EXPERT_GROUPS, tr), 0).astype(F32)
    gsel = jnp.zeros((N_EXPERT_GROUPS, tr), F32)
    cur = gs
    for _ in range(TOPK_GROUPS):
        hit = io_ng == _first_index_of_max(cur, io_ng, N_EXPERT_GROUPS)
        gsel = jnp.where(hit, 1.0, gsel)
        cur = jnp.where(hit, neg, cur)
    emask = jnp.concatenate(
        [jnp.broadcast_to(gsel[gi:gi + 1, :], (gsz, tr)) for gi in range(N_EXPERT_GROUPS)], axis=0)
    cand = jnp.where(emask > 0.0, sb, neg)
    io_e = lax.broadcasted_iota(I32, (n_experts, tr), 0).astype(F32)
    sel = jnp.zeros((n_experts, tr), F32)
    idx_rows, w_rows = [], []
    for _ in range(TOP_K):
        idx = _first_index_of_max(cand, io_e, n_experts)
        hit = io_e == idx
        w_rows.append(jnp.sum(jnp.where(hit, s, 0.0), axis=0, keepdims=True))
        idx_rows.append(idx)
        sel = jnp.where(hit, 1.0, sel)
        cand = jnp.where(hit, neg, cand)
    w = jnp.concatenate(w_rows, axis=0)
    gw_ref[...] = w / jnp.sum(w, axis=0, keepdims=True) * ROUTED_SCALE
    eidx_ref[...] = jnp.concatenate(idx_rows, axis=0).astype(I32)
    pos = carry[...] + jnp.dot(sel.astype(BF16), tri_ref[...], preferred_element_type=F32)
    rank_rows = [jnp.sum(jnp.where(io_e == idx_rows[k], pos, 0.0), axis=0, keepdims=True) for k in range(TOP_K)]
    rank_ref[...] = jnp.concatenate(rank_rows, axis=0).astype(I32)
    total = carry[...] + jnp.sum(sel, axis=1, keepdims=True)
    carry[...] = total
    cnt_ref[...] = total.astype(I32)


def _route(dm, lg, bias_col, tri):
    tr, e = dm.tr_route, dm.n_experts
    blk = lambda i: (0, i)
    return pl.pallas_call(
        functools.partial(_route_body, n_experts=e),
        name="route",
        grid=(dm.t // tr,),
        in_specs=[pl.BlockSpec((tr, ROUTER_LANES), lambda i: (i, 0)), _const_spec((e, 1)), _const_spec((tr, tr))],
        out_specs=[pl.BlockSpec((TOP_K, tr), blk), pl.BlockSpec((TOP_K, tr), blk), pl.BlockSpec((TOP_K, tr), blk),
                   pl.BlockSpec((e, 1), lambda i: (0, 0))],
        out_shape=[jax.ShapeDtypeStruct((TOP_K, dm.t), I32), jax.ShapeDtypeStruct((TOP_K, dm.t), F32),
                   jax.ShapeDtypeStruct((TOP_K, dm.t), I32), jax.ShapeDtypeStruct((e, 1), I32)],
        scratch_shapes=[pltpu.VMEM((e, 1), F32)],
        compiler_params=_cparams(("arbitrary",)),
    )(lg, bias_col, tri)


def _dispatch_body(pend_ref, padded_ref, dest_ref, x_ref, xs_ref, zbuf, zsem, sem, *, bm, n_experts):
    i = pl.program_id(0)
    tm = x_ref.shape[0]
    n_rows = xs_ref.shape[0]

    def zero_block(row_start):
        start = pl.multiple_of(row_start, bm)
        return pltpu.make_async_copy(zbuf, xs_ref.at[pl.ds(start, bm)], zsem)

    @pl.when(i == 0)
    def _():
        zbuf[...] = jnp.zeros_like(zbuf)
        used_rows = pend_ref[n_experts - 1]

        def expert_tail(e, fn):
            @pl.when(padded_ref[e] > 0)
            def _():
                fn(zero_block(pend_ref[e] - bm))

        def unused_block(b, fn):
            fn(zero_block(used_rows + b * bm))

        n_unused = (n_rows - used_rows) // bm
        for fn in (lambda c: c.start(), lambda c: c.wait()):
            lax.fori_loop(0, n_experts, lambda e, carry, fn=fn: (expert_tail(e, fn), carry)[1], 0)
            lax.fori_loop(0, n_unused, lambda b, carry, fn=fn: (unused_block(b, fn), carry)[1], 0)

    def row_copy(t, k):
        dst = dest_ref[0, 0, t * TOP_K + k]
        return pltpu.make_async_copy(x_ref.at[pl.ds(t, 1)], xs_ref.at[pl.ds(dst, 1)], sem)

    def start_tok(t, carry):
        for k in range(TOP_K):
            row_copy(t, k).start(priority=k % 2)
        return carry

    lax.fori_loop(0, tm, start_tok, 0)
    n = tm * TOP_K
    pltpu.make_async_copy(xs_ref.at[pl.ds(0, n)], xs_ref.at[pl.ds(0, n)], sem).wait()


def _dispatch(dm, pend, padded, dest_tiles, x1p):
    tm, bm, half = dm.tm_moe, dm.bm, dm.d_model // 2
    n_rows = dm.n_blocks * bm
    return pl.pallas_call(
        functools.partial(_dispatch_body, bm=bm, n_experts=dm.n_experts),
        name="dispatch",
        grid_spec=pltpu.PrefetchScalarGridSpec(
            num_scalar_prefetch=2,
            grid=(dm.t // tm,),
            in_specs=[
                pl.BlockSpec((1, 1, tm * TOP_K), lambda i, *_: (i, 0, 0), memory_space=pltpu.SMEM),
                pl.BlockSpec((tm, half), lambda i, *_: (i, 0)),
            ],
            out_specs=pl.BlockSpec(memory_space=pl.ANY),
            scratch_shapes=[pltpu.VMEM((bm, half), U32), pltpu.SemaphoreType.DMA, pltpu.SemaphoreType.DMA],
        ),
        out_shape=jax.ShapeDtypeStruct((n_rows, half), U32),
        compiler_params=pltpu.CompilerParams(dimension_semantics=("arbitrary",), has_side_effects=True,
                                             vmem_limit_bytes=V7X_VMEM_LIMIT_BYTES),
    )(pend, padded, dest_tiles, x1p)


def _experts_body(be_ref, nu_ref, xs_ref, wg_ref, wu_ref, wd_ref, ys_ref, wg_b, wu_b, wd_b):
    b = pl.program_id(0)

    @pl.when(b < nu_ref[0])
    def _():
        @pl.when(jnp.logical_or(b == 0, be_ref[b] != be_ref[jnp.maximum(b - 1, 0)]))
        def _():
            wg_b[...] = wg_ref[0, 0].astype(BF16)
            wu_b[...] = wu_ref[0, 0].astype(BF16)
            wd_b[...] = wd_ref[0, 0].astype(BF16)

        lo, hi = _unpack_bf16_pair(xs_ref[...])
        lo, hi = lo.astype(BF16), hi.astype(BF16)
        half = lo.shape[1]

        def proj(w_ref):
            return (jnp.dot(lo, w_ref[:half, :], preferred_element_type=F32)
                    + jnp.dot(hi, w_ref[half:, :], preferred_element_type=F32))

        act = jax.nn.silu(proj(wg_b)) * proj(wu_b)
        y = jnp.dot(act.astype(BF16), wd_b[...], preferred_element_type=F32)
        ys_ref[...] = _pack_bf16_pair(y[:, :half], y[:, half:])

    @pl.when(b >= nu_ref[0])
    def _():
        ys_ref[...] = jnp.zeros_like(ys_ref)


def _experts(dm, layer, block_e, n_used, xs, wg, wu, wd):
    bm, d, ed, half = dm.bm, dm.d_model, dm.expert_dim, dm.d_model // 2

    def blk(b, be, nu):
        return jnp.minimum(b, nu[0] - 1)

    def wspec(rows, cols):
        return pl.BlockSpec((1, 1, rows, cols), lambda b, be, nu: (layer, be[blk(b, be, nu)], 0, 0))

    return pl.pallas_call(
        _experts_body,
        name="experts",
        grid_spec=pltpu.PrefetchScalarGridSpec(
            num_scalar_prefetch=2,
            grid=(dm.n_blocks,),
            in_specs=[
                pl.BlockSpec((bm, half), lambda b, be, nu: (blk(b, be, nu), 0)),
                wspec(d, ed), wspec(d, ed), wspec(ed, d),
            ],
            out_specs=pl.BlockSpec((bm, half), lambda b, be, nu: (b, 0)),
            scratch_shapes=[pltpu.VMEM((d, ed), BF16), pltpu.VMEM((d, ed), BF16), pltpu.VMEM((ed, d), BF16)],
        ),
        out_shape=jax.ShapeDtypeStruct((dm.n_blocks * bm, half), U32),
        compiler_params=_cparams(("arbitrary",)),
    )(block_e, n_used, xs, wg, wu, wd)


def _combine_body(dest_ref, x1_ref, gw_ref, ys_ref, wgs_ref, wus_ref, wds_ref, g_ref, b_ref,
                  xf_ref, xb_ref, gbuf, sem, *, alpha):
    tm = x1_ref.shape[0]

    def row_copy(t, k):
        src = dest_ref[0, 0, t * TOP_K + k]
        return pltpu.make_async_copy(ys_ref.at[pl.ds(src, 1)], gbuf.at[k, pl.ds(t, 1)], sem)

    def start_tok(t, carry):
        for k in range(TOP_K):
            row_copy(t, k).start(priority=k % 2)
        return carry

    lax.fori_loop(0, tm, start_tok, 0)
    x1 = x1_ref[...]
    xb = x1.astype(BF16)
    act = (jax.nn.silu(jnp.dot(xb, wgs_ref[...], preferred_element_type=F32))
           * jnp.dot(xb, wus_ref[...], preferred_element_type=F32))
    shared = jnp.dot(act.astype(BF16), wds_ref[...], preferred_element_type=F32)
    pltpu.make_async_copy(gbuf, gbuf, sem).wait()
    half = x1.shape[1] // 2
    gw = gw_ref[...]
    r_lo = jnp.zeros((tm, half), F32)
    r_hi = jnp.zeros((tm, half), F32)
    for k in range(TOP_K):
        lo, hi = _unpack_bf16_pair(gbuf[k])
        wk = gw[:, k:k + 1]
        r_lo = r_lo + lo * wk
        r_hi = r_hi + hi * wk
    f = shared + jnp.concatenate([r_lo, r_hi], axis=1)
    x2 = _layer_norm(alpha * x1 + f, g_ref[...], b_ref[...])
    xf_ref[...] = x2
    xb_ref[...] = x2.astype(BF16)


def _combine(dm, dest_tiles, x1f, gw_tok, ys, wgs, wus, wds, g2, b2):
    tm, d, half = dm.tm_moe, dm.d_model, dm.d_model // 2
    row = lambda i, *_: (i, 0)

    def const(shape):
        nd = len(shape)
        return pl.BlockSpec(shape, lambda i, *_: (0,) * nd, pipeline_mode=pl.Buffered(1))

    return pl.pallas_call(
        functools.partial(_combine_body, alpha=dm.alpha),
        name="combine",
        grid_spec=pltpu.PrefetchScalarGridSpec(
            num_scalar_prefetch=0,
            grid=(dm.t // tm,),
            in_specs=[
                pl.BlockSpec((1, 1, tm * TOP_K), lambda i: (i, 0, 0), memory_space=pltpu.SMEM),
                pl.BlockSpec((tm, d), row),
                pl.BlockSpec((tm, TOP_K), row),
                pl.BlockSpec(memory_space=pl.ANY),
                const(wgs.shape), const(wus.shape), const(wds.shape), const(g2.shape), const(b2.shape),
            ],
            out_specs=[pl.BlockSpec((tm, d), row), pl.BlockSpec((tm, d), row)],
            scratch_shapes=[pltpu.VMEM((TOP_K, tm, half), U32), pltpu.SemaphoreType.DMA],
        ),
        out_shape=[jax.ShapeDtypeStruct((dm.t, d), F32), jax.ShapeDtypeStruct((dm.t, d), BF16)],
        compiler_params=_cparams(("arbitrary",)),
    )(dest_tiles, x1f, gw_tok, ys, wgs, wus, wds, g2, b2)


def _column_order(dm):
    aw, kvw, pw, d = dm.attn_width, dm.kv_width, dm.pool_width, dm.d_model
    q = np.array([HEAD_DIM * (GQA_GROUP * h + g) + e
                  for g in range(GQA_GROUP) for h in range(N_KV_HEADS) for e in range(HEAD_DIM)])
    k0, v0, u0 = aw, aw + kvw, aw + 2 * kvw
    ga0, gb0 = u0 + pw, u0 + pw + d
    order = np.concatenate([q, np.arange(u0, u0 + pw), np.arange(ga0, ga0 + d), np.arange(gb0, gb0 + d),
                            np.arange(k0, k0 + kvw), np.arange(v0, v0 + kvw)])
    return q, order


def _moe_plan(dm, eidx, rank, counts):
    bm, e = dm.bm, dm.n_experts
    counts = counts.reshape(e)
    padded = (counts + bm - 1) // bm * bm
    pend = jnp.cumsum(padded)
    pstart = pend - padded
    onehot = eidx[None, :, :] == jnp.arange(e, dtype=I32)[:, None, None]
    dest = jnp.sum(jnp.where(onehot, pstart[:, None, None], 0), axis=0) + rank
    dest_tiles = dest.T.reshape(dm.t // dm.tm_moe, 1, dm.tm_moe * TOP_K)
    block_start = jnp.arange(dm.n_blocks, dtype=I32) * bm
    block_e = jnp.minimum(jnp.sum(block_start[:, None] >= pend[None, :], axis=1), e - 1).astype(I32)
    n_used = (pend[-1] // bm).astype(I32).reshape(1)
    return pend.astype(I32), padded.astype(I32), dest_tiles.astype(I32), block_e, n_used


def _forward(dm, x_prompt, x_sample, cache_k, cache_v, state_pool, ln_in_g, ln_in_b, w_in, sinks, w_pool,
             pool_scale, w_branch_a, w_branch_b, w_out, ln1_g, ln1_b, w_router, router_bias, w_gate_e, w_up_e,
             w_down_e, w_gate_s, w_up_s, w_down_s, ln2_g, ln2_b):
    d, aw, kvw, pw, e = dm.d_model, dm.attn_width, dm.kv_width, dm.pool_width, dm.n_experts
    qperm, order = _column_order(dm)
    row2 = lambda v: v.reshape(1, -1)
    tri = (jnp.arange(dm.tr_route)[:, None] < jnp.arange(dm.tr_route)[None, :]).astype(BF16)
    sink_heads = np.array([GQA_GROUP * h + g for g in range(GQA_GROUP) for h in range(N_KV_HEADS)])
    kcol0 = aw + pw + 2 * d

    xf, xb = _ln_in(dm, x_prompt.reshape(dm.tp, d), x_sample.reshape(dm.ts, d), row2(ln_in_g), row2(ln_in_b))
    outs = [[] for _ in range(6)]
    for l in range(dm.depth):
        z = _in_proj(dm, xb, w_in[l][:, order].astype(BF16))
        sink_col = jnp.broadcast_to(jnp.repeat(sinks[l][sink_heads], CHUNK).reshape(aw, 1), (aw, LANES))
        a_p = _attn_prompt(dm, z, sink_col)
        a_s = _attn_sample(dm, z, cache_k[l].reshape(dm.dec_batch * dm.window, kvw),
                           cache_v[l].reshape(dm.dec_batch * dm.window, kvw), sink_col)
        hist = jnp.pad(state_pool[l], ((0, 0), (1, 0), (0, 0))).reshape(dm.dec_batch * POOL_HIST_ROWS, pw)
        d_p = _pool_prompt(dm, z)
        d_s = _pool_sample(dm, z, hist)
        wr = jnp.pad(w_router[l], ((0, 0), (0, ROUTER_LANES - e))).astype(BF16)
        x1f, x1p, lg = _post(dm, a_p, a_s, d_p, d_s, z, xf, w_pool[l].astype(BF16), row2(pool_scale[l]),
                              w_branch_a[l][qperm, :].astype(BF16), w_branch_b[l].astype(BF16),
                              w_out[l].astype(BF16), row2(ln1_g[l]), row2(ln1_b[l]), wr)
        eidx, gwt, rank, counts = _route(dm, lg, router_bias[l].reshape(e, 1), tri)
        pend, padded, dest_tiles, block_e, n_used = _moe_plan(dm, eidx, rank, counts)
        xs = _dispatch(dm, pend, padded, dest_tiles, x1p)
        ys = _experts(dm, l, block_e, n_used, xs, w_gate_e, w_up_e, w_down_e)
        xf, xb = _combine(dm, dest_tiles, x1f, gwt.T, ys, w_gate_s[l].astype(BF16), w_up_s[l].astype(BF16),
                          w_down_s[l].astype(BF16), row2(ln2_g[l]), row2(ln2_b[l]))
        def prompt_tail(col0, width, n_rows):
            return jnp.stack([z[(b + 1) * dm.seq - n_rows:(b + 1) * dm.seq, col0:col0 + width]
                              for b in range(dm.batch)])

        def sample_cols(col0, width):
            return z[dm.tp:, col0:col0 + width].reshape(dm.dec_batch, dm.dec_seq, width)

        kv_shape = (N_KV_HEADS, HEAD_DIM)
        outs[0].append(prompt_tail(kcol0, kvw, dm.window).reshape(dm.batch, dm.window, *kv_shape))
        outs[1].append(prompt_tail(kcol0 + kvw, kvw, dm.window).reshape(dm.batch, dm.window, *kv_shape))
        outs[2].append(prompt_tail(aw, pw, 15))
        ks = sample_cols(kcol0, kvw).reshape(dm.dec_batch, dm.dec_seq, *kv_shape)
        vs = sample_cols(kcol0 + kvw, kvw).reshape(dm.dec_batch, dm.dec_seq, *kv_shape)
        outs[3].append(jnp.concatenate([cache_k[l], ks], axis=1)[:, -dm.window:])
        outs[4].append(jnp.concatenate([cache_v[l], vs], axis=1)[:, -dm.window:])
        outs[5].append(jnp.concatenate([state_pool[l], sample_cols(aw, pw)], axis=1)[:, -15:])
    y_prompt = xf[:dm.tp].reshape(dm.batch, dm.seq, d)
    y_sample = xf[dm.tp:].reshape(dm.dec_batch, dm.dec_seq, d)
    return (y_prompt, y_sample) + tuple(jnp.stack(o) for o in outs)


def kernel(x_prompt, x_sample, cache_k, cache_v, state_pool, ln_in_g, ln_in_b, w_in, sinks, w_pool, pool_scale,
           w_branch_a, w_branch_b, w_out, ln1_g, ln1_b, w_router, router_bias, w_gate_e, w_up_e, w_down_e,
           w_gate_s, w_up_s, w_down_s, ln2_g, ln2_b):
    dm = Dims()
    return _forward(dm, x_prompt, x_sample, cache_k, cache_v, state_pool, ln_in_g, ln_in_b, w_in, sinks, w_pool,
                    pool_scale, w_branch_a, w_branch_b, w_out, ln1_g, ln1_b, w_router, router_bias, w_gate_e,
                    w_up_e, w_down_e, w_gate_s, w_up_s, w_down_s, ln2_g, ln2_b)
```

```python
import dataclasses
import functools

import numpy as np
import jax
import jax.numpy as jnp
from jax import lax
from jax.experimental import pallas as pl
from jax.experimental.pallas import tpu as pltpu

F32 = jnp.float32
BF16 = jnp.bfloat16
U32 = jnp.uint32
I32 = jnp.int32

LN_EPS = 1e-5
HEAD_DIM = 64
CHUNK = 64
N_KV_HEADS = 4
GQA_GROUP = 4
WINDOW_CHUNKS = 2
POOL_WINDOWS = (2, 4, 8, 16)
POOL_HIST_ROWS = 16
N_EXPERT_GROUPS = 8
TOPK_GROUPS = 4
TOP_K = 8
ROUTED_SCALE = 2.5
LANES = 128
ROUTER_LANES = LANES
ATTN_KEY_PAD = 2 * LANES
V7X_VMEM_LIMIT_BYTES = 56 * 1024 * 1024


@dataclasses.dataclass(frozen=True)
class Dims:
    d_model: int = 2048
    batch: int = 8
    seq: int = 2048
    depth: int = 4
    dec_batch: int = 32
    dec_seq: int = 64
    past_len: int = 2048
    window: int = 128
    pool_width: int = 1024
    n_experts: int = 64
    expert_dim: int = 512
    shared_dim: int = 512
    tm_ln: int = 512
    tm_proj: int = 1024
    tn_proj: int = 512
    attn_chunks: int = 4
    tr_pool: int = 512
    tm_post: int = 256
    tr_route: int = 512
    tm_disp: int = 512
    tm_moe: int = 256
    bm: int = 256

    @property
    def attn_width(self):
        return N_KV_HEADS * GQA_GROUP * HEAD_DIM

    @property
    def kv_width(self):
        return N_KV_HEADS * HEAD_DIM

    @property
    def in_width(self):
        return self.attn_width + 2 * self.kv_width + self.pool_width + 2 * self.d_model

    @property
    def tp(self):
        return self.batch * self.seq

    @property
    def ts(self):
        return self.dec_batch * self.dec_seq

    @property
    def t(self):
        return self.tp + self.ts

    @property
    def row_lines(self):
        return self.d_model // 2 // LANES

    @property
    def n_blocks(self):
        return self.t * TOP_K // self.bm + self.n_experts

    @property
    def alpha(self):
        return (2 * self.depth) ** 0.25


def _cparams(sem):
    return pltpu.CompilerParams(dimension_semantics=sem, vmem_limit_bytes=V7X_VMEM_LIMIT_BYTES)


def _const_spec(shape):
    nd = len(shape)
    return pl.BlockSpec(shape, lambda *_: (0,) * nd, pipeline_mode=pl.Buffered(1))


def _layer_norm(h, g, b):
    mu = jnp.mean(h, axis=-1, keepdims=True)
    c = h - mu
    var = jnp.mean(c * c, axis=-1, keepdims=True)
    return c * lax.rsqrt(var + LN_EPS) * g + b


def _pack_bf16_pair(lo, hi):
    lo_bits = lax.bitcast_convert_type(lo.astype(BF16).astype(F32), U32)
    hi_bits = lax.bitcast_convert_type(hi.astype(BF16).astype(F32), U32)
    return (lo_bits >> 16) | hi_bits


def _store_row_tiles(ref, x):
    n_chunks = x.shape[1] // LANES
    for j in range(n_chunks):
        ref[pl.ds(j, x.shape[0], stride=n_chunks), :] = x[:, j * LANES:(j + 1) * LANES]


def _load_row_tile_chunk(ref, j, rows, n_chunks):
    return ref[pl.ds(j, rows, stride=n_chunks), :]


def _unpack_bf16_pair(p):
    lo = lax.bitcast_convert_type(p << 16, F32)
    hi = lax.bitcast_convert_type(p & jnp.uint32(0xFFFF0000), F32)
    return lo, hi


def _ln_in_body(xp_ref, xs_ref, g_ref, b_ref, xf_ref, xb_ref, *, n_prompt_tiles):
    i = pl.program_id(0)

    def emit(x):
        y = _layer_norm(x, g_ref[...], b_ref[...])
        xf_ref[...] = y
        xb_ref[...] = y.astype(BF16)

    @pl.when(i < n_prompt_tiles)
    def _():
        emit(xp_ref[...])

    @pl.when(i >= n_prompt_tiles)
    def _():
        emit(xs_ref[...])


def _ln_in(dm, xp, xs, g, b):
    tm, d = dm.tm_ln, dm.d_model
    npt, nst = dm.tp // tm, dm.ts // tm
    return pl.pallas_call(
        functools.partial(_ln_in_body, n_prompt_tiles=npt),
        name="ln_in",
        grid=(npt + nst,),
        in_specs=[
            pl.BlockSpec((tm, d), lambda i: (jnp.minimum(i, npt - 1), 0)),
            pl.BlockSpec((tm, d), lambda i: (jnp.maximum(i - npt, 0), 0)),
            _const_spec((1, d)),
            _const_spec((1, d)),
        ],
        out_specs=[pl.BlockSpec((tm, d), lambda i: (i, 0)), pl.BlockSpec((tm, d), lambda i: (i, 0))],
        out_shape=[jax.ShapeDtypeStruct((dm.t, d), F32), jax.ShapeDtypeStruct((dm.t, d), BF16)],
        compiler_params=_cparams(("arbitrary",)),
    )(xp, xs, g, b)


def _in_proj_body(x_ref, w_ref, z_ref):
    z_ref[...] = jnp.dot(x_ref[...], w_ref[...], preferred_element_type=F32)


def _in_proj(dm, xb, w):
    tm, tn, d, n = dm.tm_proj, dm.tn_proj, dm.d_model, dm.in_width
    return pl.pallas_call(
        _in_proj_body,
        name="in_proj",
        grid=(dm.t // tm, n // tn),
        in_specs=[pl.BlockSpec((tm, d), lambda i, j: (i, 0)), pl.BlockSpec((d, tn), lambda i, j: (0, j))],
        out_specs=pl.BlockSpec((tm, tn), lambda i, j: (i, j)),
        out_shape=jax.ShapeDtypeStruct((dm.t, n), F32),
        compiler_params=_cparams(("arbitrary", "arbitrary")),
    )(xb, w)


def _attn_chunk(q, k, v_ext, sink, n_missing):
    kvw = N_KV_HEADS * HEAD_DIM
    n_keys = (WINDOW_CHUNKS + 1) * CHUNK
    lane_head = lax.broadcasted_iota(I32, (CHUNK, kvw), 1) >> 6
    blocks = []
    for g in range(GQA_GROUP):
        slab = q[:, g * kvw:(g + 1) * kvw]
        for h in range(N_KV_HEADS):
            blocks.append(jnp.where(lane_head == h, slab, 0.0).astype(BF16))
    lhs = jnp.concatenate(blocks, axis=0)
    s = lax.dot_general(lhs, k, (((1,), (1,)), ((), ())), preferred_element_type=F32)
    col = lax.broadcasted_iota(I32, s.shape, 1)
    valid = col < n_keys if n_missing is None else jnp.logical_and(col < n_keys, col >= n_missing)
    s = jnp.where(valid, s, -jnp.inf)
    lanes = sink.shape[1]
    s_lo, s_hi = s[:, :lanes], s[:, lanes:]
    row_max = jnp.max(jnp.maximum(s_lo, s_hi), axis=-1, keepdims=True)
    m = jnp.maximum(jnp.broadcast_to(row_max, sink.shape), sink)
    p = jnp.concatenate([jnp.exp(s_lo - m), jnp.exp(s_hi - m)], axis=1).astype(BF16)
    o_ext = jnp.dot(p, v_ext, preferred_element_type=F32)
    denom = o_ext[:, kvw:] + jnp.exp(sink - m)
    inv = 1.0 / denom
    o = o_ext[:, :kvw] * jnp.concatenate([inv] * (kvw // lanes), axis=1)
    outs = []
    for g in range(GQA_GROUP):
        acc = jnp.zeros((CHUNK, kvw), F32)
        for h in range(N_KV_HEADS):
            rb = g * N_KV_HEADS + h
            acc = acc + jnp.where(lane_head == h, o[rb * CHUNK:(rb + 1) * CHUNK, :], 0.0)
        outs.append(acc.astype(BF16))
    return outs


def _attn_body(q_ref, kh_ref, kc_ref, vh_ref, vc_ref, sink_ref, o_ref, *, nch, banded):
    kvw = N_KV_HEADS * HEAD_DIM
    hist = WINDOW_CHUNKS * CHUNK
    pad = ATTN_KEY_PAD - hist - CHUNK
    sink = sink_ref[...]

    def with_ones(v):
        return jnp.concatenate([v, jnp.ones((v.shape[0], sink.shape[1]), BF16)], axis=1)

    kc = kc_ref[...].astype(BF16)
    vc = with_ones(vc_ref[...].astype(BF16))
    kh = kh_ref[...].astype(BF16)
    vh = with_ones(vh_ref[...].astype(BF16))
    kpad = jnp.zeros((pad, kvw), BF16)
    vpad = jnp.zeros((pad, vc.shape[1]), BF16)
    if banded:
        c0 = pl.program_id(1) * nch
        kall = jnp.concatenate([kh, kc, kpad], axis=0)
        vall = jnp.concatenate([vh, vc, vpad], axis=0)
    for j in range(nch):
        q = q_ref[j * CHUNK:(j + 1) * CHUNK, :] * (HEAD_DIM ** -0.5)
        if banded:
            k = kall[j * CHUNK:j * CHUNK + ATTN_KEY_PAD, :]
            v = vall[j * CHUNK:j * CHUNK + ATTN_KEY_PAD, :]
            n_missing = jnp.maximum(WINDOW_CHUNKS - c0 - j, 0) * CHUNK if j < WINDOW_CHUNKS else None
        else:
            k = jnp.concatenate([kh[j * hist:(j + 1) * hist, :], kc[j * CHUNK:(j + 1) * CHUNK, :], kpad], axis=0)
            v = jnp.concatenate([vh[j * hist:(j + 1) * hist, :], vc[j * CHUNK:(j + 1) * CHUNK, :], vpad], axis=0)
            n_missing = None
        outs = _attn_chunk(q, k, v, sink, n_missing)
        for g in range(GQA_GROUP):
            o_ref[j * CHUNK:(j + 1) * CHUNK, g * kvw:(g + 1) * kvw] = outs[g]


def _attn_prompt(dm, z, sink_col):
    nch = dm.attn_chunks
    rows, hist = nch * CHUNK, WINDOW_CHUNKS * CHUNK
    ns = dm.seq // rows
    aw, kvw = dm.attn_width, dm.kv_width
    kcol = (aw + dm.pool_width + 2 * dm.d_model) // kvw
    vcol = kcol + 1
    assert rows % hist == 0 and dm.seq % rows == 0

    def hist_spec(colblk):
        return pl.BlockSpec((hist, kvw), lambda b, i: (jnp.maximum((b * dm.seq + i * rows) // hist - 1, 0), colblk))

    def cur_spec(colblk):
        return pl.BlockSpec((rows, kvw), lambda b, i: (b * ns + i, colblk))

    return pl.pallas_call(
        functools.partial(_attn_body, nch=nch, banded=True),
        name="attn_prompt",
        grid=(dm.batch, ns),
        in_specs=[
            pl.BlockSpec((rows, aw), lambda b, i: (b * ns + i, 0)),
            hist_spec(kcol), cur_spec(kcol), hist_spec(vcol), cur_spec(vcol),
            _const_spec((aw, LANES)),
        ],
        out_specs=pl.BlockSpec((rows, aw), lambda b, i: (b * ns + i, 0)),
        out_shape=jax.ShapeDtypeStruct((dm.tp, aw), BF16),
        compiler_params=_cparams(("arbitrary", "arbitrary")),
    )(z, z, z, z, z, sink_col)


def _attn_sample(dm, z, ck, cv, sink_col):
    nch = dm.attn_chunks
    rows, hist = nch * CHUNK, WINDOW_CHUNKS * CHUNK
    aw, kvw = dm.attn_width, dm.kv_width
    kcol = (aw + dm.pool_width + 2 * dm.d_model) // kvw
    vcol = kcol + 1
    row0 = dm.tp // rows
    assert dm.dec_seq == CHUNK and dm.window == hist and dm.dec_batch % nch == 0 and dm.tp % rows == 0
    hist_spec = pl.BlockSpec((nch * hist, kvw), lambda b: (b, 0))

    def new_spec(colblk):
        return pl.BlockSpec((rows, kvw), lambda b: (row0 + b, colblk))

    return pl.pallas_call(
        functools.partial(_attn_body, nch=nch, banded=False),
        name="attn_sample",
        grid=(dm.dec_batch // nch,),
        in_specs=[
            pl.BlockSpec((rows, aw), lambda b: (row0 + b, 0)),
            hist_spec, new_spec(kcol), hist_spec, new_spec(vcol),
            _const_spec((aw, LANES)),
        ],
        out_specs=pl.BlockSpec((rows, aw), lambda b: (b, 0)),
        out_shape=jax.ShapeDtypeStruct((dm.ts, aw), BF16),
        compiler_params=_cparams(("arbitrary",)),
    )(z, ck, z, cv, z, sink_col)


def _pool_body(prev_ref, u_ref, o_ref, buf, *, tr, pos0, zero_first):
    i = pl.program_id(1)
    hr = POOL_HIST_ROWS
    prev = prev_ref[...]
    if zero_first:
        prev = jnp.where(i == 0, 0.0, prev)
    buf[0:hr, :] = prev
    buf[hr:hr + tr, :] = u_ref[...]
    pos = lax.broadcasted_iota(I32, (tr, 1), 0) + (pos0 + i * tr)
    gw = u_ref.shape[1] // len(POOL_WINDOWS)
    for g, w in enumerate(POOL_WINDOWS):
        cs = slice(g * gw, (g + 1) * gw)
        u = buf[hr:hr + tr, cs]
        acc = u
        for j in range(1, w):
            acc = acc + buf[hr - j:hr - j + tr, cs]
        cnt = jnp.minimum(pos + 1, w).astype(F32)
        o_ref[:, cs] = (acc / cnt - u).astype(BF16)


def _pool_prompt(dm, z):
    tr, pw, hr = dm.tr_pool, dm.pool_width, POOL_HIST_ROWS
    ucol = dm.attn_width // pw
    nt = dm.seq // tr
    return pl.pallas_call(
        functools.partial(_pool_body, tr=tr, pos0=0, zero_first=True),
        name="pool_prompt",
        grid=(dm.batch, nt),
        in_specs=[
            pl.BlockSpec((hr, pw), lambda b, i: (jnp.maximum((b * dm.seq + i * tr) // hr - 1, 0), ucol)),
            pl.BlockSpec((tr, pw), lambda b, i: (b * nt + i, ucol)),
        ],
        out_specs=pl.BlockSpec((tr, pw), lambda b, i: (b * nt + i, 0)),
        out_shape=jax.ShapeDtypeStruct((dm.tp, pw), BF16),
        scratch_shapes=[pltpu.VMEM((hr + tr, pw), F32)],
        compiler_params=_cparams(("arbitrary", "arbitrary")),
    )(z, z)


def _pool_sample(dm, z, hist):
    tr, pw, hr = dm.dec_seq, dm.pool_width, POOL_HIST_ROWS
    ucol = dm.attn_width // pw
    row0 = dm.tp // tr
    return pl.pallas_call(
        functools.partial(_pool_body, tr=tr, pos0=dm.past_len, zero_first=False),
        name="pool_sample",
        grid=(dm.dec_batch, 1),
        in_specs=[
            pl.BlockSpec((hr, pw), lambda b, i: (b, 0)),
            pl.BlockSpec((tr, pw), lambda b, i: (row0 + b, ucol)),
        ],
        out_specs=pl.BlockSpec((tr, pw), lambda b, i: (b, 0)),
        out_shape=jax.ShapeDtypeStruct((dm.ts, pw), BF16),
        scratch_shapes=[pltpu.VMEM((hr + tr, pw), F32)],
        compiler_params=_cparams(("arbitrary", "arbitrary")),
    )(hist, z)


def _post_body(ap_ref, as_ref, dp_ref, ds_ref, ga_ref, gb_ref, x_ref, wpool_ref, pscale_ref, wba_ref, wbb_ref,
               wout_ref, g_ref, b_ref, wr_ref, x1_ref, x1p_ref, lg_ref, *, alpha, n_prompt_tiles):
    is_prompt = pl.program_id(0) < n_prompt_tiles
    a = jnp.where(is_prompt, ap_ref[...], as_ref[...])
    d = jnp.where(is_prompt, dp_ref[...], ds_ref[...])
    n_groups = wpool_ref.shape[0]
    gw = d.shape[1] // n_groups
    pb = jnp.concatenate(
        [jnp.dot(d[:, g * gw:(g + 1) * gw], wpool_ref[g], preferred_element_type=F32) for g in range(n_groups)],
        axis=1)
    pb = (pb * pscale_ref[...]).astype(BF16)
    br_a = jnp.dot(a, wba_ref[...], preferred_element_type=F32)
    br_b = jnp.dot(pb, wbb_ref[...], preferred_element_type=F32)
    merged = jax.nn.sigmoid(ga_ref[...]) * br_a + jax.nn.sigmoid(gb_ref[...]) * br_b
    y = jnp.dot(merged.astype(BF16), wout_ref[...], preferred_element_type=F32)
    x1 = _layer_norm(alpha * x_ref[...] + y, g_ref[...], b_ref[...])
    x1_ref[...] = x1
    half = x1.shape[1] // 2
    _store_row_tiles(x1p_ref, _pack_bf16_pair(x1[:, :half], x1[:, half:]))
    lg_ref[...] = jnp.dot(x1.astype(BF16), wr_ref[...], preferred_element_type=F32)


def _post(dm, a_p, a_s, d_p, d_s, z, xf, wpool, pscale, wba, wbb, wout, g1, b1, wr):
    tm, dmod, aw, pw = dm.tm_post, dm.d_model, dm.attn_width, dm.pool_width
    gcol = (aw + pw) // dmod
    assert (aw + pw) % dmod == 0
    npt = dm.tp // tm
    row = lambda i: (i, 0)
    prow = lambda i: (jnp.minimum(i, npt - 1), 0)
    srow = lambda i: (jnp.maximum(i - npt, 0), 0)
    return pl.pallas_call(
        functools.partial(_post_body, alpha=dm.alpha, n_prompt_tiles=npt),
        name="post_mixer",
        grid=(dm.t // tm,),
        in_specs=[
            pl.BlockSpec((tm, aw), prow),
            pl.BlockSpec((tm, aw), srow),
            pl.BlockSpec((tm, pw), prow),
            pl.BlockSpec((tm, pw), srow),
            pl.BlockSpec((tm, dmod), lambda i: (i, gcol)),
            pl.BlockSpec((tm, dmod), lambda i: (i, gcol + 1)),
            pl.BlockSpec((tm, dmod), row),
            _const_spec(wpool.shape), _const_spec(pscale.shape), _const_spec(wba.shape),
            _const_spec(wbb.shape), _const_spec(wout.shape), _const_spec(g1.shape), _const_spec(b1.shape),
            _const_spec(wr.shape),
        ],
        out_specs=[
            pl.BlockSpec((tm, dmod), row),
            pl.BlockSpec((tm * dm.row_lines, LANES), row),
            pl.BlockSpec((tm, ROUTER_LANES), row),
        ],
        out_shape=[
            jax.ShapeDtypeStruct((dm.t, dmod), F32),
            jax.ShapeDtypeStruct((dm.t * dm.row_lines, LANES), U32),
            jax.ShapeDtypeStruct((dm.t, ROUTER_LANES), F32),
        ],
        compiler_params=_cparams(("arbitrary",)),
    )(a_p, a_s, d_p, d_s, z, z, xf, wpool, pscale, wba, wbb, wout, g1, b1, wr)


def _first_index_of_max(x, iota, n):
    mx = jnp.max(x, axis=0, keepdims=True)
    return jnp.min(jnp.where(x == mx, iota, n), axis=0, keepdims=True)


def _route_body(lg_ref, bias_ref, tri_ref, eidx_ref, gw_ref, rank_ref, cnt_ref, carry, *, n_experts):
    i = pl.program_id(0)

    @pl.when(i == 0)
    def _():
        carry[...] = jnp.zeros_like(carry)

    tr = lg_ref.shape[0]
    gsz = n_experts // N_EXPERT_GROUPS
    neg = -jnp.inf
    s = jax.nn.sigmoid(lg_ref[...].T[0:n_experts, :])
    sb = s + bias_ref[...]
    io_g = lax.broadcasted_iota(I32, (gsz, tr), 0).astype(F32)
    gs_rows = []
    for gi in range(N_EXPERT_GROUPS):
        blk = sb[gi * gsz:(gi + 1) * gsz, :]
        m1 = jnp.max(blk, axis=0, keepdims=True)
        first = _first_index_of_max(blk, io_g, gsz)
        m2 = jnp.max(jnp.where(io_g == first, neg, blk), axis=0, keepdims=True)
        gs_rows.append(m1 + m2)
    gs = jnp.concatenate(gs_rows, axis=0)
    io_ng = lax.broadcasted_iota(I32, (N_EXPERT_GROUPS, tr), 0).astype(F32)
    gsel = jnp.zeros((N_EXPERT_GROUPS, tr), F32)
    cur = gs
    for _ in range(TOPK_GROUPS):
        hit = io_ng == _first_index_of_max(cur, io_ng, N_EXPERT_GROUPS)
        gsel = jnp.where(hit, 1.0, gsel)
        cur = jnp.where(hit, neg, cur)
    emask = jnp.concatenate(
        [jnp.broadcast_to(gsel[gi:gi + 1, :], (gsz, tr)) for gi in range(N_EXPERT_GROUPS)], axis=0)
    cand = jnp.where(emask > 0.0, sb, neg)
    io_e = lax.broadcasted_iota(I32, (n_experts, tr), 0).astype(F32)
    sel = jnp.zeros((n_experts, tr), F32)
    idx_rows, w_rows = [], []
    for _ in range(TOP_K):
        idx = _first_index_of_max(cand, io_e, n_experts)
        hit = io_e == idx
        w_rows.append(jnp.sum(jnp.where(hit, s, 0.0), axis=0, keepdims=True))
        idx_rows.append(idx)
        sel = jnp.where(hit, 1.0, sel)
        cand = jnp.where(hit, neg, cand)
    w = jnp.concatenate(w_rows, axis=0)
    gw_ref[...] = w / jnp.sum(w, axis=0, keepdims=True) * ROUTED_SCALE
    eidx_ref[...] = jnp.concatenate(idx_rows, axis=0).astype(I32)
    pos = carry[...] + jnp.dot(sel.astype(BF16), tri_ref[...], preferred_element_type=F32)
    rank_rows = [jnp.sum(jnp.where(io_e == idx_rows[k], pos, 0.0), axis=0, keepdims=True) for k in range(TOP_K)]
    rank_ref[...] = jnp.concatenate(rank_rows, axis=0).astype(I32)
    total = carry[...] + jnp.sum(sel, axis=1, keepdims=True)
    carry[...] = total
    cnt_ref[...] = total.astype(I32)


def _route(dm, lg, bias_col, tri):
    tr, e = dm.tr_route, dm.n_experts
    blk = lambda i: (0, i)
    return pl.pallas_call(
        functools.partial(_route_body, n_experts=e),
        name="route",
        grid=(dm.t // tr,),
        in_specs=[pl.BlockSpec((tr, ROUTER_LANES), lambda i: (i, 0)), _const_spec((e, 1)), _const_spec((tr, tr))],
        out_specs=[pl.BlockSpec((TOP_K, tr), blk), pl.BlockSpec((TOP_K, tr), blk), pl.BlockSpec((TOP_K, tr), blk),
                   pl.BlockSpec((e, 1), lambda i: (0, 0))],
        out_shape=[jax.ShapeDtypeStruct((TOP_K, dm.t), I32), jax.ShapeDtypeStruct((TOP_K, dm.t), F32),
                   jax.ShapeDtypeStruct((TOP_K, dm.t), I32), jax.ShapeDtypeStruct((e, 1), I32)],
        scratch_shapes=[pltpu.VMEM((e, 1), F32)],
        compiler_params=_cparams(("arbitrary",)),
    )(lg, bias_col, tri)


def _dispatch_body(pend_ref, padded_ref, dest_ref, x_ref, xs_ref, zbuf, zsem, sem, *, bm, rl, n_experts):
    i = pl.program_id(0)
    tm = x_ref.shape[0] // rl
    n_rows = xs_ref.shape[0] // rl

    def zero_block(row_start):
        start = pl.multiple_of(row_start * rl, bm * rl)
        return pltpu.make_async_copy(zbuf, xs_ref.at[pl.ds(start, bm * rl)], zsem)

    @pl.when(i == 0)
    def _():
        zbuf[...] = jnp.zeros_like(zbuf)
        used_rows = pend_ref[n_experts - 1]

        def expert_tail(e, fn):
            @pl.when(padded_ref[e] > 0)
            def _():
                fn(zero_block(pend_ref[e] - bm))

        def unused_block(b, fn):
            fn(zero_block(used_rows + b * bm))

        n_unused = (n_rows - used_rows) // bm
        for fn in (lambda c: c.start(), lambda c: c.wait()):
            lax.fori_loop(0, n_experts, lambda e, carry, fn=fn: (expert_tail(e, fn), carry)[1], 0)
            lax.fori_loop(0, n_unused, lambda b, carry, fn=fn: (unused_block(b, fn), carry)[1], 0)

    def start_tok(t, carry):
        src = x_ref.at[pl.ds(pl.multiple_of(t * rl, rl), rl)]
        for k in range(TOP_K):
            dst = pl.multiple_of(dest_ref[0, 0, t * TOP_K + k] * rl, rl)
            pltpu.make_async_copy(src, xs_ref.at[pl.ds(dst, rl)], sem).start(
                priority=k % 2)
        return carry

    lax.fori_loop(0, tm, start_tok, 0)
    n = tm * TOP_K * rl
    pltpu.make_async_copy(xs_ref.at[pl.ds(0, n)], xs_ref.at[pl.ds(0, n)], sem).wait()


def _dispatch(dm, pend, padded, dest_tiles, x1p):
    tm, bm, rl = dm.tm_disp, dm.bm, dm.row_lines
    n_rows = dm.n_blocks * bm
    dest_tiles = dest_tiles.reshape(dm.t // tm, 1, tm * TOP_K)
    return pl.pallas_call(
        functools.partial(_dispatch_body, bm=bm, rl=rl, n_experts=dm.n_experts),
        name="dispatch",
        grid_spec=pltpu.PrefetchScalarGridSpec(
            num_scalar_prefetch=2,
            grid=(dm.t // tm,),
            in_specs=[
                pl.BlockSpec((1, 1, tm * TOP_K), lambda i, *_: (i, 0, 0), memory_space=pltpu.SMEM),
                pl.BlockSpec((tm * rl, LANES), lambda i, *_: (i, 0)),
            ],
            out_specs=pl.BlockSpec(memory_space=pl.ANY),
            scratch_shapes=[pltpu.VMEM((bm * rl, LANES), U32), pltpu.SemaphoreType.DMA, pltpu.SemaphoreType.DMA],
        ),
        out_shape=jax.ShapeDtypeStruct((n_rows * rl, LANES), U32),
        compiler_params=pltpu.CompilerParams(dimension_semantics=("arbitrary",), has_side_effects=True,
                                             vmem_limit_bytes=V7X_VMEM_LIMIT_BYTES),
    )(pend, padded, dest_tiles, x1p)


def _experts_body(be_ref, nu_ref, xs_ref, wg_ref, wu_ref, wd_ref, ys_ref, wg_b, wu_b, wd_b):
    b = pl.program_id(0)

    @pl.when(b < nu_ref[0])
    def _():
        @pl.when(jnp.logical_or(b == 0, be_ref[b] != be_ref[jnp.maximum(b - 1, 0)]))
        def _():
            wg_b[...] = wg_ref[0, 0].astype(BF16)
            wu_b[...] = wu_ref[0, 0].astype(BF16)
            wd_b[...] = wd_ref[0, 0].astype(BF16)

        rl = wg_b.shape[0] // 2 // LANES
        bm = xs_ref.shape[0] // rl
        xs = jnp.concatenate([_load_row_tile_chunk(xs_ref, j, bm, rl) for j in range(rl)], axis=1)
        lo, hi = _unpack_bf16_pair(xs)
        lo, hi = lo.astype(BF16), hi.astype(BF16)
        half = lo.shape[1]

        def proj(w_ref):
            return (jnp.dot(lo, w_ref[:half, :], preferred_element_type=F32)
                    + jnp.dot(hi, w_ref[half:, :], preferred_element_type=F32))

        act = jax.nn.silu(proj(wg_b)) * proj(wu_b)
        y = jnp.dot(act.astype(BF16), wd_b[...], preferred_element_type=F32)
        _store_row_tiles(ys_ref, _pack_bf16_pair(y[:, :half], y[:, half:]))

    @pl.when(b >= nu_ref[0])
    def _():
        ys_ref[...] = jnp.zeros_like(ys_ref)


def _experts(dm, layer, block_e, n_used, xs, wg, wu, wd):
    bm, d, ed, rl = dm.bm, dm.d_model, dm.expert_dim, dm.row_lines

    def blk(b, be, nu):
        return jnp.minimum(b, nu[0] - 1)

    def wspec(rows, cols):
        return pl.BlockSpec((1, 1, rows, cols), lambda b, be, nu: (layer, be[blk(b, be, nu)], 0, 0))

    return pl.pallas_call(
        _experts_body,
        name="experts",
        grid_spec=pltpu.PrefetchScalarGridSpec(
            num_scalar_prefetch=2,
            grid=(dm.n_blocks,),
            in_specs=[
                pl.BlockSpec((bm * rl, LANES), lambda b, be, nu: (blk(b, be, nu), 0)),
                wspec(d, ed), wspec(d, ed), wspec(ed, d),
            ],
            out_specs=pl.BlockSpec((bm * rl, LANES), lambda b, be, nu: (b, 0)),
            scratch_shapes=[pltpu.VMEM((d, ed), BF16), pltpu.VMEM((d, ed), BF16), pltpu.VMEM((ed, d), BF16)],
        ),
        out_shape=jax.ShapeDtypeStruct((dm.n_blocks * bm * rl, LANES), U32),
        compiler_params=_cparams(("arbitrary",)),
    )(block_e, n_used, xs, wg, wu, wd)


def _combine_body(dest_ref, x1_ref, gw_ref, ys_ref, wgs_ref, wus_ref, wds_ref, g_ref, b_ref,
                  xf_ref, xb_ref, gbuf, sem, *, alpha):
    tm = x1_ref.shape[0]
    rl = gbuf.shape[1] // tm

    def start_tok(t, carry):
        land = pl.multiple_of(t * rl, rl)
        for k in range(TOP_K):
            src = pl.multiple_of(dest_ref[0, 0, t * TOP_K + k] * rl, rl)
            pltpu.make_async_copy(ys_ref.at[pl.ds(src, rl)], gbuf.at[k, pl.ds(land, rl)], sem).start(
                priority=k % 2)
        return carry

    lax.fori_loop(0, tm, start_tok, 0)
    x1 = x1_ref[...]
    xb = x1.astype(BF16)
    act = (jax.nn.silu(jnp.dot(xb, wgs_ref[...], preferred_element_type=F32))
           * jnp.dot(xb, wus_ref[...], preferred_element_type=F32))
    shared = jnp.dot(act.astype(BF16), wds_ref[...], preferred_element_type=F32)
    pltpu.make_async_copy(gbuf, gbuf, sem).wait()
    gw = gw_ref[...]
    wks = [jnp.broadcast_to(gw[:, k:k + 1], (tm, LANES)) for k in range(TOP_K)]
    r_lo, r_hi = [], []
    for j in range(rl):
        a_lo = jnp.zeros((tm, LANES), F32)
        a_hi = jnp.zeros((tm, LANES), F32)
        for k in range(TOP_K):
            lo, hi = _unpack_bf16_pair(_load_row_tile_chunk(gbuf.at[k], j, tm, rl))
            a_lo = a_lo + lo * wks[k]
            a_hi = a_hi + hi * wks[k]
        r_lo.append(a_lo)
        r_hi.append(a_hi)
    f = shared + jnp.concatenate(r_lo + r_hi, axis=1)
    x2 = _layer_norm(alpha * x1 + f, g_ref[...], b_ref[...])
    xf_ref[...] = x2
    xb_ref[...] = x2.astype(BF16)


def _combine(dm, dest_tiles, x1f, gw_tok, ys, wgs, wus, wds, g2, b2):
    tm, d, rl = dm.tm_moe, dm.d_model, dm.row_lines
    row = lambda i, *_: (i, 0)

    def const(shape):
        nd = len(shape)
        return pl.BlockSpec(shape, lambda i, *_: (0,) * nd, pipeline_mode=pl.Buffered(1))

    return pl.pallas_call(
        functools.partial(_combine_body, alpha=dm.alpha),
        name="combine",
        grid_spec=pltpu.PrefetchScalarGridSpec(
            num_scalar_prefetch=0,
            grid=(dm.t // tm,),
            in_specs=[
                pl.BlockSpec((1, 1, tm * TOP_K), lambda i: (i, 0, 0), memory_space=pltpu.SMEM),
                pl.BlockSpec((tm, d), row),
                pl.BlockSpec((tm, TOP_K), row),
                pl.BlockSpec(memory_space=pl.ANY),
                const(wgs.shape), const(wus.shape), const(wds.shape), const(g2.shape), const(b2.shape),
            ],
            out_specs=[pl.BlockSpec((tm, d), row), pl.BlockSpec((tm, d), row)],
            scratch_shapes=[pltpu.VMEM((TOP_K, tm * rl, LANES), U32), pltpu.SemaphoreType.DMA],
        ),
        out_shape=[jax.ShapeDtypeStruct((dm.t, d), F32), jax.ShapeDtypeStruct((dm.t, d), BF16)],
        compiler_params=_cparams(("arbitrary",)),
    )(dest_tiles, x1f, gw_tok, ys, wgs, wus, wds, g2, b2)


def _group_major(w, axis):
    shape = w.shape
    w = w.reshape(shape[:axis] + (N_KV_HEADS, GQA_GROUP, HEAD_DIM) + shape[axis + 1:])
    return jnp.swapaxes(w, axis, axis + 1).reshape(shape)


def _arrange_w_in(dm, w):
    aw, kvw, pw, d = dm.attn_width, dm.kv_width, dm.pool_width, dm.d_model
    k0, v0, u0 = aw, aw + kvw, aw + 2 * kvw
    return jnp.concatenate([_group_major(w[:, :aw], 1), w[:, u0:], w[:, k0:v0], w[:, v0:u0]], axis=1).astype(BF16)


def _moe_plan(dm, eidx, rank, counts):
    bm, e = dm.bm, dm.n_experts
    counts = counts.reshape(e)
    padded = (counts + bm - 1) // bm * bm
    pend = jnp.cumsum(padded)
    pstart = pend - padded
    onehot = eidx[None, :, :] == jnp.arange(e, dtype=I32)[:, None, None]
    dest = jnp.sum(jnp.where(onehot, pstart[:, None, None], 0), axis=0) + rank
    dest_tiles = dest.T.reshape(dm.t // dm.tm_moe, 1, dm.tm_moe * TOP_K)
    block_start = jnp.arange(dm.n_blocks, dtype=I32) * bm
    block_e = jnp.minimum(jnp.sum(block_start[:, None] >= pend[None, :], axis=1), e - 1).astype(I32)
    n_used = (pend[-1] // bm).astype(I32).reshape(1)
    return pend.astype(I32), padded.astype(I32), dest_tiles.astype(I32), block_e, n_used


def _forward(dm, x_prompt, x_sample, cache_k, cache_v, state_pool, ln_in_g, ln_in_b, w_in, sinks, w_pool,
             pool_scale, w_branch_a, w_branch_b, w_out, ln1_g, ln1_b, w_router, router_bias, w_gate_e, w_up_e,
             w_down_e, w_gate_s, w_up_s, w_down_s, ln2_g, ln2_b):
    d, aw, kvw, pw, e = dm.d_model, dm.attn_width, dm.kv_width, dm.pool_width, dm.n_experts
    row2 = lambda v: v.reshape(1, -1)
    tri = (jnp.arange(dm.tr_route)[:, None] < jnp.arange(dm.tr_route)[None, :]).astype(BF16)
    sink_heads = np.array([GQA_GROUP * h + g for g in range(GQA_GROUP) for h in range(N_KV_HEADS)])
    kcol0 = aw + pw + 2 * d

    xf, xb = _ln_in(dm, x_prompt.reshape(dm.tp, d), x_sample.reshape(dm.ts, d), row2(ln_in_g), row2(ln_in_b))
    outs = [[] for _ in range(6)]
    for l in range(dm.depth):
        z = _in_proj(dm, xb, _arrange_w_in(dm, w_in[l]))
        sink_col = jnp.broadcast_to(jnp.repeat(sinks[l][sink_heads], CHUNK).reshape(aw, 1), (aw, LANES))
        a_p = _attn_prompt(dm, z, sink_col)
        a_s = _attn_sample(dm, z, cache_k[l].reshape(dm.dec_batch * dm.window, kvw),
                           cache_v[l].reshape(dm.dec_batch * dm.window, kvw), sink_col)
        hist = jnp.pad(state_pool[l], ((0, 0), (1, 0), (0, 0))).reshape(dm.dec_batch * POOL_HIST_ROWS, pw)
        d_p = _pool_prompt(dm, z)
        d_s = _pool_sample(dm, z, hist)
        wr = jnp.pad(w_router[l], ((0, 0), (0, ROUTER_LANES - e))).astype(BF16)
        x1f, x1p, lg = _post(dm, a_p, a_s, d_p, d_s, z, xf, w_pool[l].astype(BF16), row2(pool_scale[l]),
                              _group_major(w_branch_a[l], 0).astype(BF16), w_branch_b[l].astype(BF16),
                              w_out[l].astype(BF16), row2(ln1_g[l]), row2(ln1_b[l]), wr)
        eidx, gwt, rank, counts = _route(dm, lg, router_bias[l].reshape(e, 1), tri)
        pend, padded, dest_tiles, block_e, n_used = _moe_plan(dm, eidx, rank, counts)
        xs = _dispatch(dm, pend, padded, dest_tiles, x1p)
        ys = _experts(dm, l, block_e, n_used, xs, w_gate_e, w_up_e, w_down_e)
        xf, xb = _combine(dm, dest_tiles, x1f, gwt.T, ys, w_gate_s[l].astype(BF16), w_up_s[l].astype(BF16),
                          w_down_s[l].astype(BF16), row2(ln2_g[l]), row2(ln2_b[l]))
        def prompt_tail(col0, width, n_rows):
            return jnp.stack([z[(b + 1) * dm.seq - n_rows:(b + 1) * dm.seq, col0:col0 + width]
                              for b in range(dm.batch)])

        def sample_cols(col0, width):
            return z[dm.tp:, col0:col0 + width].reshape(dm.dec_batch, dm.dec_seq, width)

        kv_shape = (N_KV_HEADS, HEAD_DIM)
        outs[0].append(prompt_tail(kcol0, kvw, dm.window).reshape(dm.batch, dm.window, *kv_shape))
        outs[1].append(prompt_tail(kcol0 + kvw, kvw, dm.window).reshape(dm.batch, dm.window, *kv_shape))
        outs[2].append(prompt_tail(aw, pw, 15))
        ks = sample_cols(kcol0, kvw).reshape(dm.dec_batch, dm.dec_seq, *kv_shape)
        vs = sample_cols(kcol0 + kvw, kvw).reshape(dm.dec_batch, dm.dec_seq, *kv_shape)
        outs[3].append(jnp.concatenate([cache_k[l], ks], axis=1)[:, -dm.window:])
        outs[4].append(jnp.concatenate([cache_v[l], vs], axis=1)[:, -dm.window:])
        outs[5].append(jnp.concatenate([state_pool[l], sample_cols(aw, pw)], axis=1)[:, -15:])
    y_prompt = xf[:dm.tp].reshape(dm.batch, dm.seq, d)
    y_sample = xf[dm.tp:].reshape(dm.dec_batch, dm.dec_seq, d)
    return (y_prompt, y_sample) + tuple(jnp.stack(o) for o in outs)


def kernel(x_prompt, x_sample, cache_k, cache_v, state_pool, ln_in_g, ln_in_b, w_in, sinks, w_pool, pool_scale,
           w_branch_a, w_branch_b, w_out, ln1_g, ln1_b, w_router, router_bias, w_gate_e, w_up_e, w_down_e,
           w_gate_s, w_up_s, w_down_s, ln2_g, ln2_b):
    dm = Dims()
    return _forward(dm, x_prompt, x_sample, cache_k, cache_v, state_pool, ln_in_g, ln_in_b, w_in, sinks, w_pool,
                    pool_scale, w_branch_a, w_branch_b, w_out, ln1_g, ln1_b, w_router, router_bias, w_gate_e,
                    w_up_e, w_down_e, w_gate_s, w_up_s, w_down_s, ln2_g, ln2_b)
```

```python
import dataclasses
import functools

import numpy as np
import jax
import jax.numpy as jnp
from jax import lax
from jax.experimental import pallas as pl
from jax.experimental.pallas import tpu as pltpu

F32 = jnp.float32
BF16 = jnp.bfloat16
U32 = jnp.uint32
I32 = jnp.int32

LN_EPS = 1e-5
HEAD_DIM = 64
CHUNK = 64
N_KV_HEADS = 4
GQA_GROUP = 4
WINDOW_CHUNKS = 2
POOL_WINDOWS = (2, 4, 8, 16)
POOL_HIST_ROWS = 16
N_EXPERT_GROUPS = 8
TOPK_GROUPS = 4
TOP_K = 8
ROUTED_SCALE = 2.5
LANES = 128
ROUTER_LANES = LANES
ATTN_KEY_PAD = 2 * LANES
V7X_VMEM_LIMIT_BYTES = 56 * 1024 * 1024


@dataclasses.dataclass(frozen=True)
class Dims:
    d_model: int = 2048
    batch: int = 8
    seq: int = 2048
    depth: int = 4
    dec_batch: int = 32
    dec_seq: int = 64
    past_len: int = 2048
    window: int = 128
    pool_width: int = 1024
    n_experts: int = 64
    expert_dim: int = 512
    shared_dim: int = 512
    tm_ln: int = 512
    tm_proj: int = 1024
    tn_proj: int = 512
    attn_chunks: int = 4
    tr_pool: int = 512
    tm_post: int = 256
    tr_route: int = 512
    tm_disp: int = 512
    tm_moe: int = 256
    bm: int = 256

    @property
    def attn_width(self):
        return N_KV_HEADS * GQA_GROUP * HEAD_DIM

    @property
    def kv_width(self):
        return N_KV_HEADS * HEAD_DIM

    @property
    def in_width(self):
        return self.attn_width + 2 * self.kv_width + self.pool_width + 2 * self.d_model

    @property
    def tp(self):
        return self.batch * self.seq

    @property
    def ts(self):
        return self.dec_batch * self.dec_seq

    @property
    def t(self):
        return self.tp + self.ts

    @property
    def row_lines(self):
        return self.d_model // 2 // LANES

    @property
    def n_blocks(self):
        return self.t * TOP_K // self.bm + self.n_experts

    @property
    def alpha(self):
        return (2 * self.depth) ** 0.25


def _cparams(sem):
    return pltpu.CompilerParams(dimension_semantics=sem, vmem_limit_bytes=V7X_VMEM_LIMIT_BYTES)


def _const_spec(shape):
    nd = len(shape)
    return pl.BlockSpec(shape, lambda *_: (0,) * nd, pipeline_mode=pl.Buffered(1))


def _layer_norm(h, g, b):
    mu = jnp.mean(h, axis=-1, keepdims=True)
    c = h - mu
    var = jnp.mean(c * c, axis=-1, keepdims=True)
    return c * lax.rsqrt(var + LN_EPS) * g + b


def _pack_bf16_pair(lo, hi):
    lo_bits = lax.bitcast_convert_type(lo.astype(BF16).astype(F32), U32)
    hi_bits = lax.bitcast_convert_type(hi.astype(BF16).astype(F32), U32)
    return (lo_bits >> 16) | hi_bits


def _store_row_tiles(ref, x):
    n_chunks = x.shape[1] // LANES
    for j in range(n_chunks):
        ref[pl.ds(j, x.shape[0], stride=n_chunks), :] = x[:, j * LANES:(j + 1) * LANES]


def _load_row_tile_chunk(ref, j, rows, n_chunks):
    return ref[pl.ds(j, rows, stride=n_chunks), :]


def _unpack_bf16_pair(p):
    lo = lax.bitcast_convert_type(p << 16, F32)
    hi = lax.bitcast_convert_type(p & jnp.uint32(0xFFFF0000), F32)
    return lo, hi


def _ln_in_body(xp_ref, xs_ref, g_ref, b_ref, xf_ref, xb_ref, *, n_prompt_tiles):
    i = pl.program_id(0)

    def emit(x):
        y = _layer_norm(x, g_ref[...], b_ref[...])
        xf_ref[...] = y
        xb_ref[...] = y.astype(BF16)

    @pl.when(i < n_prompt_tiles)
    def _():
        emit(xp_ref[...])

    @pl.when(i >= n_prompt_tiles)
    def _():
        emit(xs_ref[...])


def _ln_in(dm, xp, xs, g, b):
    tm, d = dm.tm_ln, dm.d_model
    npt, nst = dm.tp // tm, dm.ts // tm
    return pl.pallas_call(
        functools.partial(_ln_in_body, n_prompt_tiles=npt),
        name="ln_in",
        grid=(npt + nst,),
        in_specs=[
            pl.BlockSpec((tm, d), lambda i: (jnp.minimum(i, npt - 1), 0)),
            pl.BlockSpec((tm, d), lambda i: (jnp.maximum(i - npt, 0), 0)),
            _const_spec((1, d)),
            _const_spec((1, d)),
        ],
        out_specs=[pl.BlockSpec((tm, d), lambda i: (i, 0)), pl.BlockSpec((tm, d), lambda i: (i, 0))],
        out_shape=[jax.ShapeDtypeStruct((dm.t, d), F32), jax.ShapeDtypeStruct((dm.t, d), BF16)],
        compiler_params=_cparams(("arbitrary",)),
    )(xp, xs, g, b)


def _in_proj_body(x_ref, w_ref, z_ref):
    z_ref[...] = jnp.dot(x_ref[...], w_ref[...], preferred_element_type=F32)


def _in_proj(dm, xb, w):
    tm, tn, d, n = dm.tm_proj, dm.tn_proj, dm.d_model, dm.in_width
    return pl.pallas_call(
        _in_proj_body,
        name="in_proj",
        grid=(dm.t // tm, n // tn),
        in_specs=[pl.BlockSpec((tm, d), lambda i, j: (i, 0)), pl.BlockSpec((d, tn), lambda i, j: (0, j))],
        out_specs=pl.BlockSpec((tm, tn), lambda i, j: (i, j)),
        out_shape=jax.ShapeDtypeStruct((dm.t, n), F32),
        compiler_params=_cparams(("arbitrary", "arbitrary")),
    )(xb, w)


def _attn_chunk(q, k, v_ext, sink, n_missing):
    kvw = N_KV_HEADS * HEAD_DIM
    n_keys = (WINDOW_CHUNKS + 1) * CHUNK
    lane_head = lax.broadcasted_iota(I32, (CHUNK, kvw), 1) >> 6
    blocks = []
    for g in range(GQA_GROUP):
        slab = q[:, g * kvw:(g + 1) * kvw]
        for h in range(N_KV_HEADS):
            blocks.append(jnp.where(lane_head == h, slab, 0.0).astype(BF16))
    lhs = jnp.concatenate(blocks, axis=0)
    s = lax.dot_general(lhs, k, (((1,), (1,)), ((), ())), preferred_element_type=F32)
    col = lax.broadcasted_iota(I32, s.shape, 1)
    valid = col < n_keys if n_missing is None else jnp.logical_and(col < n_keys, col >= n_missing)
    s = jnp.where(valid, s, -jnp.inf)
    lanes = sink.shape[1]
    s_lo, s_hi = s[:, :lanes], s[:, lanes:]
    row_max = jnp.max(jnp.maximum(s_lo, s_hi), axis=-1, keepdims=True)
    m = jnp.maximum(jnp.broadcast_to(row_max, sink.shape), sink)
    p = jnp.concatenate([jnp.exp(s_lo - m), jnp.exp(s_hi - m)], axis=1).astype(BF16)
    o_ext = jnp.dot(p, v_ext, preferred_element_type=F32)
    denom = o_ext[:, kvw:] + jnp.exp(sink - m)
    inv = 1.0 / denom
    o = o_ext[:, :kvw] * jnp.concatenate([inv] * (kvw // lanes), axis=1)
    outs = []
    for g in range(GQA_GROUP):
        acc = jnp.zeros((CHUNK, kvw), F32)
        for h in range(N_KV_HEADS):
            rb = g * N_KV_HEADS + h
            acc = acc + jnp.where(lane_head == h, o[rb * CHUNK:(rb + 1) * CHUNK, :], 0.0)
        outs.append(acc.astype(BF16))
    return outs


def _attn_body(q_ref, kh_ref, kc_ref, vh_ref, vc_ref, sink_ref, o_ref, *, nch, banded):
    kvw = N_KV_HEADS * HEAD_DIM
    hist = WINDOW_CHUNKS * CHUNK
    pad = ATTN_KEY_PAD - hist - CHUNK
    sink = sink_ref[...]

    def with_ones(v):
        return jnp.concatenate([v, jnp.ones((v.shape[0], sink.shape[1]), BF16)], axis=1)

    kc = kc_ref[...].astype(BF16)
    vc = with_ones(vc_ref[...].astype(BF16))
    kh = kh_ref[...].astype(BF16)
    vh = with_ones(vh_ref[...].astype(BF16))
    kpad = jnp.zeros((pad, kvw), BF16)
    vpad = jnp.zeros((pad, vc.shape[1]), BF16)
    if banded:
        c0 = pl.program_id(1) * nch
        kall = jnp.concatenate([kh, kc, kpad], axis=0)
        vall = jnp.concatenate([vh, vc, vpad], axis=0)
    for j in range(nch):
        q = q_ref[j * CHUNK:(j + 1) * CHUNK, :] * (HEAD_DIM ** -0.5)
        if banded:
            k = kall[j * CHUNK:j * CHUNK + ATTN_KEY_PAD, :]
            v = vall[j * CHUNK:j * CHUNK + ATTN_KEY_PAD, :]
            n_missing = jnp.maximum(WINDOW_CHUNKS - c0 - j, 0) * CHUNK if j < WINDOW_CHUNKS else None
        else:
            k = jnp.concatenate([kh[j * hist:(j + 1) * hist, :], kc[j * CHUNK:(j + 1) * CHUNK, :], kpad], axis=0)
            v = jnp.concatenate([vh[j * hist:(j + 1) * hist, :], vc[j * CHUNK:(j + 1) * CHUNK, :], vpad], axis=0)
            n_missing = None
        outs = _attn_chunk(q, k, v, sink, n_missing)
        for g in range(GQA_GROUP):
            o_ref[j * CHUNK:(j + 1) * CHUNK, g * kvw:(g + 1) * kvw] = outs[g]


def _attn_prompt(dm, z, sink_col):
    nch = dm.attn_chunks
    rows, hist = nch * CHUNK, WINDOW_CHUNKS * CHUNK
    ns = dm.seq // rows
    aw, kvw = dm.attn_width, dm.kv_width
    kcol = (aw + dm.pool_width + 2 * dm.d_model) // kvw
    vcol = kcol + 1
    assert rows % hist == 0 and dm.seq % rows == 0

    def hist_spec(colblk):
        return pl.BlockSpec((hist, kvw), lambda b, i: (jnp.maximum((b * dm.seq + i * rows) // hist - 1, 0), colblk))

    def cur_spec(colblk):
        return pl.BlockSpec((rows, kvw), lambda b, i: (b * ns + i, colblk))

    return pl.pallas_call(
        functools.partial(_attn_body, nch=nch, banded=True),
        name="attn_prompt",
        grid=(dm.batch, ns),
        in_specs=[
            pl.BlockSpec((rows, aw), lambda b, i: (b * ns + i, 0)),
            hist_spec(kcol), cur_spec(kcol), hist_spec(vcol), cur_spec(vcol),
            _const_spec((aw, LANES)),
        ],
        out_specs=pl.BlockSpec((rows, aw), lambda b, i: (b * ns + i, 0)),
        out_shape=jax.ShapeDtypeStruct((dm.tp, aw), BF16),
        compiler_params=_cparams(("arbitrary", "arbitrary")),
    )(z, z, z, z, z, sink_col)


def _attn_sample(dm, z, ck, cv, sink_col):
    nch = dm.attn_chunks
    rows, hist = nch * CHUNK, WINDOW_CHUNKS * CHUNK
    aw, kvw = dm.attn_width, dm.kv_width
    kcol = (aw + dm.pool_width + 2 * dm.d_model) // kvw
    vcol = kcol + 1
    row0 = dm.tp // rows
    assert dm.dec_seq == CHUNK and dm.window == hist and dm.dec_batch % nch == 0 and dm.tp % rows == 0
    hist_spec = pl.BlockSpec((nch * hist, kvw), lambda b: (b, 0))

    def new_spec(colblk):
        return pl.BlockSpec((rows, kvw), lambda b: (row0 + b, colblk))

    return pl.pallas_call(
        functools.partial(_attn_body, nch=nch, banded=False),
        name="attn_sample",
        grid=(dm.dec_batch // nch,),
        in_specs=[
            pl.BlockSpec((rows, aw), lambda b: (row0 + b, 0)),
            hist_spec, new_spec(kcol), hist_spec, new_spec(vcol),
            _const_spec((aw, LANES)),
        ],
        out_specs=pl.BlockSpec((rows, aw), lambda b: (b, 0)),
        out_shape=jax.ShapeDtypeStruct((dm.ts, aw), BF16),
        compiler_params=_cparams(("arbitrary",)),
    )(z, ck, z, cv, z, sink_col)


def _pool_body(prev_ref, u_ref, o_ref, buf, *, tr, pos0, zero_first):
    i = pl.program_id(1)
    hr = POOL_HIST_ROWS
    prev = prev_ref[...]
    if zero_first:
        prev = jnp.where(i == 0, 0.0, prev)
    buf[0:hr, :] = prev
    buf[hr:hr + tr, :] = u_ref[...]
    pos = lax.broadcasted_iota(I32, (tr, 1), 0) + (pos0 + i * tr)
    gw = u_ref.shape[1] // len(POOL_WINDOWS)
    for g, w in enumerate(POOL_WINDOWS):
        cs = slice(g * gw, (g + 1) * gw)
        u = buf[hr:hr + tr, cs]
        acc = u
        for j in range(1, w):
            acc = acc + buf[hr - j:hr - j + tr, cs]
        cnt = jnp.minimum(pos + 1, w).astype(F32)
        o_ref[:, cs] = (acc / cnt - u).astype(BF16)


def _pool_prompt(dm, z):
    tr, pw, hr = dm.tr_pool, dm.pool_width, POOL_HIST_ROWS
    ucol = dm.attn_width // pw
    nt = dm.seq // tr
    return pl.pallas_call(
        functools.partial(_pool_body, tr=tr, pos0=0, zero_first=True),
        name="pool_prompt",
        grid=(dm.batch, nt),
        in_specs=[
            pl.BlockSpec((hr, pw), lambda b, i: (jnp.maximum((b * dm.seq + i * tr) // hr - 1, 0), ucol)),
            pl.BlockSpec((tr, pw), lambda b, i: (b * nt + i, ucol)),
        ],
        out_specs=pl.BlockSpec((tr, pw), lambda b, i: (b * nt + i, 0)),
        out_shape=jax.ShapeDtypeStruct((dm.tp, pw), BF16),
        scratch_shapes=[pltpu.VMEM((hr + tr, pw), F32)],
        compiler_params=_cparams(("arbitrary", "arbitrary")),
    )(z, z)


def _pool_sample(dm, z, hist):
    tr, pw, hr = dm.dec_seq, dm.pool_width, POOL_HIST_ROWS
    ucol = dm.attn_width // pw
    row0 = dm.tp // tr
    return pl.pallas_call(
        functools.partial(_pool_body, tr=tr, pos0=dm.past_len, zero_first=False),
        name="pool_sample",
        grid=(dm.dec_batch, 1),
        in_specs=[
            pl.BlockSpec((hr, pw), lambda b, i: (b, 0)),
            pl.BlockSpec((tr, pw), lambda b, i: (row0 + b, ucol)),
        ],
        out_specs=pl.BlockSpec((tr, pw), lambda b, i: (b, 0)),
        out_shape=jax.ShapeDtypeStruct((dm.ts, pw), BF16),
        scratch_shapes=[pltpu.VMEM((hr + tr, pw), F32)],
        compiler_params=_cparams(("arbitrary", "arbitrary")),
    )(hist, z)


def _post_body(ap_ref, as_ref, dp_ref, ds_ref, ga_ref, gb_ref, x_ref, wpool_ref, pscale_ref, wba_ref, wbb_ref,
               wout_ref, g_ref, b_ref, wr_ref, x1_ref, x1p_ref, lg_ref, *, alpha, n_prompt_tiles):
    is_prompt = pl.program_id(0) < n_prompt_tiles
    a = jnp.where(is_prompt, ap_ref[...], as_ref[...])
    d = jnp.where(is_prompt, dp_ref[...], ds_ref[...])
    n_groups = wpool_ref.shape[0]
    gw = d.shape[1] // n_groups
    pb = jnp.concatenate(
        [jnp.dot(d[:, g * gw:(g + 1) * gw], wpool_ref[g], preferred_element_type=F32) for g in range(n_groups)],
        axis=1)
    pb = (pb * pscale_ref[...]).astype(BF16)
    br_a = jnp.dot(a, wba_ref[...], preferred_element_type=F32)
    br_b = jnp.dot(pb, wbb_ref[...], preferred_element_type=F32)
    merged = jax.nn.sigmoid(ga_ref[...]) * br_a + jax.nn.sigmoid(gb_ref[...]) * br_b
    y = jnp.dot(merged.astype(BF16), wout_ref[...], preferred_element_type=F32)
    x1 = _layer_norm(alpha * x_ref[...] + y, g_ref[...], b_ref[...])
    x1_ref[...] = x1
    half = x1.shape[1] // 2
    _store_row_tiles(x1p_ref, _pack_bf16_pair(x1[:, :half], x1[:, half:]))
    lg_ref[...] = jnp.dot(x1.astype(BF16), wr_ref[...], preferred_element_type=F32)


def _post(dm, a_p, a_s, d_p, d_s, z, xf, wpool, pscale, wba, wbb, wout, g1, b1, wr):
    tm, dmod, aw, pw = dm.tm_post, dm.d_model, dm.attn_width, dm.pool_width
    gcol = (aw + pw) // dmod
    assert (aw + pw) % dmod == 0
    npt = dm.tp // tm
    row = lambda i: (i, 0)
    prow = lambda i: (jnp.minimum(i, npt - 1), 0)
    srow = lambda i: (jnp.maximum(i - npt, 0), 0)
    return pl.pallas_call(
        functools.partial(_post_body, alpha=dm.alpha, n_prompt_tiles=npt),
        name="post_mixer",
        grid=(dm.t // tm,),
        in_specs=[
            pl.BlockSpec((tm, aw), prow),
            pl.BlockSpec((tm, aw), srow),
            pl.BlockSpec((tm, pw), prow),
            pl.BlockSpec((tm, pw), srow),
            pl.BlockSpec((tm, dmod), lambda i: (i, gcol)),
            pl.BlockSpec((tm, dmod), lambda i: (i, gcol + 1)),
            pl.BlockSpec((tm, dmod), row),
            _const_spec(wpool.shape), _const_spec(pscale.shape), _const_spec(wba.shape),
            _const_spec(wbb.shape), _const_spec(wout.shape), _const_spec(g1.shape), _const_spec(b1.shape),
            _const_spec(wr.shape),
        ],
        out_specs=[
            pl.BlockSpec((tm, dmod), row),
            pl.BlockSpec((tm * dm.row_lines, LANES), row),
            pl.BlockSpec((tm, ROUTER_LANES), row),
        ],
        out_shape=[
            jax.ShapeDtypeStruct((dm.t, dmod), F32),
            jax.ShapeDtypeStruct((dm.t * dm.row_lines, LANES), U32),
            jax.ShapeDtypeStruct((dm.t, ROUTER_LANES), F32),
        ],
        compiler_params=_cparams(("arbitrary",)),
    )(a_p, a_s, d_p, d_s, z, z, xf, wpool, pscale, wba, wbb, wout, g1, b1, wr)


def _first_index_of_max(x, iota, n):
    mx = jnp.max(x, axis=0, keepdims=True)
    return jnp.min(jnp.where(x == mx, iota, n), axis=0, keepdims=True)


def _route_body(lg_ref, bias_ref, tri_ref, eidx_ref, gw_ref, rank_ref, cnt_ref, carry, *, n_experts):
    i = pl.program_id(0)

    @pl.when(i == 0)
    def _():
        carry[...] = jnp.zeros_like(carry)

    tr = lg_ref.shape[0]
    gsz = n_experts // N_EXPERT_GROUPS
    neg = -jnp.inf
    s = jax.nn.sigmoid(lg_ref[...].T[0:n_experts, :])
    sb = s + bias_ref[...]
    io_g = lax.broadcasted_iota(I32, (gsz, tr), 0).astype(F32)
    gs_rows = []
    for gi in range(N_EXPERT_GROUPS):
        blk = sb[gi * gsz:(gi + 1) * gsz, :]
        m1 = jnp.max(blk, axis=0, keepdims=True)
        first = _first_index_of_max(blk, io_g, gsz)
        m2 = jnp.max(jnp.where(io_g == first, neg, blk), axis=0, keepdims=True)
        gs_rows.append(m1 + m2)
    gs = jnp.concatenate(gs_rows, axis=0)
    io_ng = lax.broadcasted_iota(I32, (N_EXPERT_GROUPS, tr), 0).astype(F32)
    gsel = jnp.zeros((N_EXPERT_GROUPS, tr), F32)
    cur = gs
    for _ in range(TOPK_GROUPS):
        hit = io_ng == _first_index_of_max(cur, io_ng, N_EXPERT_GROUPS)
        gsel = jnp.where(hit, 1.0, gsel)
        cur = jnp.where(hit, neg, cur)
    emask = jnp.concatenate(
        [jnp.broadcast_to(gsel[gi:gi + 1, :], (gsz, tr)) for gi in range(N_EXPERT_GROUPS)], axis=0)
    cand = jnp.where(emask > 0.0, sb, neg)
    io_e = lax.broadcasted_iota(I32, (n_experts, tr), 0).astype(F32)
    sel = jnp.zeros((n_experts, tr), F32)
    idx_rows, w_rows = [], []
    for _ in range(TOP_K):
        idx = _first_index_of_max(cand, io_e, n_experts)
        hit = io_e == idx
        w_rows.append(jnp.sum(jnp.where(hit, s, 0.0), axis=0, keepdims=True))
        idx_rows.append(idx)
        sel = jnp.where(hit, 1.0, sel)
        cand = jnp.where(hit, neg, cand)
    w = jnp.concatenate(w_rows, axis=0)
    gw_ref[...] = w / jnp.sum(w, axis=0, keepdims=True) * ROUTED_SCALE
    eidx_ref[...] = jnp.concatenate(idx_rows, axis=0).astype(I32)
    pos = carry[...] + jnp.dot(sel.astype(BF16), tri_ref[...], preferred_element_type=F32)
    rank_rows = [jnp.sum(jnp.where(io_e == idx_rows[k], pos, 0.0), axis=0, keepdims=True) for k in range(TOP_K)]
    rank_ref[...] = jnp.concatenate(rank_rows, axis=0).astype(I32)
    total = carry[...] + jnp.sum(sel, axis=1, keepdims=True)
    carry[...] = total
    cnt_ref[...] = total.astype(I32)


def _route(dm, lg, bias_col, tri):
    tr, e = dm.tr_route, dm.n_experts
    blk = lambda i: (0, i)
    return pl.pallas_call(
        functools.partial(_route_body, n_experts=e),
        name="route",
        grid=(dm.t // tr,),
        in_specs=[pl.BlockSpec((tr, ROUTER_LANES), lambda i: (i, 0)), _const_spec((e, 1)), _const_spec((tr, tr))],
        out_specs=[pl.BlockSpec((TOP_K, tr), blk), pl.BlockSpec((TOP_K, tr), blk), pl.BlockSpec((TOP_K, tr), blk),
                   pl.BlockSpec((e, 1), lambda i: (0, 0))],
        out_shape=[jax.ShapeDtypeStruct((TOP_K, dm.t), I32), jax.ShapeDtypeStruct((TOP_K, dm.t), F32),
                   jax.ShapeDtypeStruct((TOP_K, dm.t), I32), jax.ShapeDtypeStruct((e, 1), I32)],
        scratch_shapes=[pltpu.VMEM((e, 1), F32)],
        compiler_params=_cparams(("arbitrary",)),
    )(lg, bias_col, tri)


def _dispatch_body(pend_ref, padded_ref, dest_ref, x_ref, xs_ref, zbuf, zsem, sem, *, bm, rl, n_experts):
    i = pl.program_id(0)
    tm = x_ref.shape[0] // rl
    n_rows = xs_ref.shape[0] // rl

    def zero_block(row_start):
        start = pl.multiple_of(row_start * rl, bm * rl)
        return pltpu.make_async_copy(zbuf, xs_ref.at[pl.ds(start, bm * rl)], zsem)

    @pl.when(i == 0)
    def _():
        zbuf[...] = jnp.zeros_like(zbuf)
        used_rows = pend_ref[n_experts - 1]

        def expert_tail(e, fn):
            @pl.when(padded_ref[e] > 0)
            def _():
                fn(zero_block(pend_ref[e] - bm))

        def unused_block(b, fn):
            fn(zero_block(used_rows + b * bm))

        n_unused = (n_rows - used_rows) // bm
        for fn in (lambda c: c.start(), lambda c: c.wait()):
            lax.fori_loop(0, n_experts, lambda e, carry, fn=fn: (expert_tail(e, fn), carry)[1], 0)
            lax.fori_loop(0, n_unused, lambda b, carry, fn=fn: (unused_block(b, fn), carry)[1], 0)

    def start_tok(t, carry):
        src = x_ref.at[pl.ds(pl.multiple_of(t * rl, rl), rl)]
        for k in range(TOP_K):
            dst = pl.multiple_of(dest_ref[0, 0, t * TOP_K + k] * rl, rl)
            pltpu.make_async_copy(src, xs_ref.at[pl.ds(dst, rl)], sem).start(
                priority=k % 2)
        return carry

    lax.fori_loop(0, tm, start_tok, 0)
    n = tm * TOP_K * rl
    pltpu.make_async_copy(xs_ref.at[pl.ds(0, n)], xs_ref.at[pl.ds(0, n)], sem).wait()


def _dispatch(dm, pend, padded, dest_tiles, x1p):
    tm, bm, rl = dm.tm_disp, dm.bm, dm.row_lines
    n_rows = dm.n_blocks * bm
    dest_tiles = dest_tiles.reshape(dm.t // tm, 1, tm * TOP_K)
    return pl.pallas_call(
        functools.partial(_dispatch_body, bm=bm, rl=rl, n_experts=dm.n_experts),
        name="dispatch",
        grid_spec=pltpu.PrefetchScalarGridSpec(
            num_scalar_prefetch=2,
            grid=(dm.t // tm,),
            in_specs=[
                pl.BlockSpec((1, 1, tm * TOP_K), lambda i, *_: (i, 0, 0), memory_space=pltpu.SMEM),
                pl.BlockSpec((tm * rl, LANES), lambda i, *_: (i, 0)),
            ],
            out_specs=pl.BlockSpec(memory_space=pl.ANY),
            scratch_shapes=[pltpu.VMEM((bm * rl, LANES), U32), pltpu.SemaphoreType.DMA, pltpu.SemaphoreType.DMA],
        ),
        out_shape=jax.ShapeDtypeStruct((n_rows * rl, LANES), U32),
        compiler_params=pltpu.CompilerParams(dimension_semantics=("arbitrary",), has_side_effects=True,
                                             vmem_limit_bytes=V7X_VMEM_LIMIT_BYTES),
    )(pend, padded, dest_tiles, x1p)


def _experts_body(be_ref, nu_ref, first_ref, slot_ref, nxt_ref, xs_ref, wg_hbm, wu_hbm, wd_hbm, ys_ref,
                  wg_f, wu_f, wd_f, wg_b, wu_b, wd_b, wsem, *, layer):
    b = pl.program_id(0)

    def fetch(e, slot):
        return [pltpu.make_async_copy(src.at[layer, e], dst.at[slot], wsem.at[slot])
                for src, dst in ((wg_hbm, wg_f), (wu_hbm, wu_f), (wd_hbm, wd_f))]

    @pl.when(b < nu_ref[0])
    def _():
        @pl.when(first_ref[b] == 1)
        def _():
            slot = slot_ref[b]

            @pl.when(b == 0)
            def _():
                for c in fetch(be_ref[0], slot):
                    c.start()

            @pl.when(nxt_ref[b] >= 0)
            def _():
                for c in fetch(nxt_ref[b], 1 - slot):
                    c.start()

            for c in fetch(be_ref[b], slot):
                c.wait()
            wg_b[...] = wg_f[slot].astype(BF16)
            wu_b[...] = wu_f[slot].astype(BF16)
            wd_b[...] = wd_f[slot].astype(BF16)

        rl = wg_b.shape[0] // 2 // LANES
        bm = xs_ref.shape[0] // rl
        xs = jnp.concatenate([_load_row_tile_chunk(xs_ref, j, bm, rl) for j in range(rl)], axis=1)
        lo, hi = _unpack_bf16_pair(xs)
        lo, hi = lo.astype(BF16), hi.astype(BF16)
        half = lo.shape[1]

        def proj(w_ref):
            return (jnp.dot(lo, w_ref[:half, :], preferred_element_type=F32)
                    + jnp.dot(hi, w_ref[half:, :], preferred_element_type=F32))

        act = jax.nn.silu(proj(wg_b)) * proj(wu_b)
        y = jnp.dot(act.astype(BF16), wd_b[...], preferred_element_type=F32)
        _store_row_tiles(ys_ref, _pack_bf16_pair(y[:, :half], y[:, half:]))

    @pl.when(b >= nu_ref[0])
    def _():
        ys_ref[...] = jnp.zeros_like(ys_ref)


def _experts(dm, layer, runs, xs, wg, wu, wd):
    bm, d, ed, rl = dm.bm, dm.d_model, dm.expert_dim, dm.row_lines
    any_spec = pl.BlockSpec(memory_space=pl.ANY)

    return pl.pallas_call(
        functools.partial(_experts_body, layer=layer),
        name="experts",
        grid_spec=pltpu.PrefetchScalarGridSpec(
            num_scalar_prefetch=5,
            grid=(dm.n_blocks,),
            in_specs=[
                pl.BlockSpec((bm * rl, LANES), lambda b, be, nu, *_: (jnp.minimum(b, nu[0] - 1), 0)),
                any_spec, any_spec, any_spec,
            ],
            out_specs=pl.BlockSpec((bm * rl, LANES), lambda b, *_: (b, 0)),
            scratch_shapes=[
                pltpu.VMEM((2, d, ed), F32), pltpu.VMEM((2, d, ed), F32), pltpu.VMEM((2, ed, d), F32),
                pltpu.VMEM((d, ed), BF16), pltpu.VMEM((d, ed), BF16), pltpu.VMEM((ed, d), BF16),
                pltpu.SemaphoreType.DMA((2,)),
            ],
        ),
        out_shape=jax.ShapeDtypeStruct((dm.n_blocks * bm * rl, LANES), U32),
        compiler_params=_cparams(("arbitrary",)),
    )(*runs, xs, wg, wu, wd)


def _combine_body(dest_ref, x1_ref, gw_ref, ys_ref, wgs_ref, wus_ref, wds_ref, g_ref, b_ref,
                  xf_ref, xb_ref, gbuf, sem, *, alpha):
    tm = x1_ref.shape[0]
    rl = gbuf.shape[1] // tm

    def start_tok(t, carry):
        land = pl.multiple_of(t * rl, rl)
        for k in range(TOP_K):
            src = pl.multiple_of(dest_ref[0, 0, t * TOP_K + k] * rl, rl)
            pltpu.make_async_copy(ys_ref.at[pl.ds(src, rl)], gbuf.at[k, pl.ds(land, rl)], sem).start(
                priority=k % 2)
        return carry

    lax.fori_loop(0, tm, start_tok, 0)
    x1 = x1_ref[...]
    xb = x1.astype(BF16)
    act = (jax.nn.silu(jnp.dot(xb, wgs_ref[...], preferred_element_type=F32))
           * jnp.dot(xb, wus_ref[...], preferred_element_type=F32))
    shared = jnp.dot(act.astype(BF16), wds_ref[...], preferred_element_type=F32)
    pltpu.make_async_copy(gbuf, gbuf, sem).wait()
    gw = gw_ref[...]
    wks = [jnp.broadcast_to(gw[:, k:k + 1], (tm, LANES)) for k in range(TOP_K)]
    r_lo, r_hi = [], []
    for j in range(rl):
        a_lo = jnp.zeros((tm, LANES), F32)
        a_hi = jnp.zeros((tm, LANES), F32)
        for k in range(TOP_K):
            lo, hi = _unpack_bf16_pair(_load_row_tile_chunk(gbuf.at[k], j, tm, rl))
            a_lo = a_lo + lo * wks[k]
            a_hi = a_hi + hi * wks[k]
        r_lo.append(a_lo)
        r_hi.append(a_hi)
    f = shared + jnp.concatenate(r_lo + r_hi, axis=1)
    x2 = _layer_norm(alpha * x1 + f, g_ref[...], b_ref[...])
    xf_ref[...] = x2
    xb_ref[...] = x2.astype(BF16)


def _combine(dm, dest_tiles, x1f, gw_tok, ys, wgs, wus, wds, g2, b2):
    tm, d, rl = dm.tm_moe, dm.d_model, dm.row_lines
    row = lambda i, *_: (i, 0)

    def const(shape):
        nd = len(shape)
        return pl.BlockSpec(shape, lambda i, *_: (0,) * nd, pipeline_mode=pl.Buffered(1))

    return pl.pallas_call(
        functools.partial(_combine_body, alpha=dm.alpha),
        name="combine",
        grid_spec=pltpu.PrefetchScalarGridSpec(
            num_scalar_prefetch=0,
            grid=(dm.t // tm,),
            in_specs=[
                pl.BlockSpec((1, 1, tm * TOP_K), lambda i: (i, 0, 0), memory_space=pltpu.SMEM),
                pl.BlockSpec((tm, d), row),
                pl.BlockSpec((tm, TOP_K), row),
                pl.BlockSpec(memory_space=pl.ANY),
                const(wgs.shape), const(wus.shape), const(wds.shape), const(g2.shape), const(b2.shape),
            ],
            out_specs=[pl.BlockSpec((tm, d), row), pl.BlockSpec((tm, d), row)],
            scratch_shapes=[pltpu.VMEM((TOP_K, tm * rl, LANES), U32), pltpu.SemaphoreType.DMA],
        ),
        out_shape=[jax.ShapeDtypeStruct((dm.t, d), F32), jax.ShapeDtypeStruct((dm.t, d), BF16)],
        compiler_params=_cparams(("arbitrary",)),
    )(dest_tiles, x1f, gw_tok, ys, wgs, wus, wds, g2, b2)


def _group_major(w, axis):
    shape = w.shape
    w = w.reshape(shape[:axis] + (N_KV_HEADS, GQA_GROUP, HEAD_DIM) + shape[axis + 1:])
    return jnp.swapaxes(w, axis, axis + 1).reshape(shape)


def _arrange_w_in(dm, w):
    aw, kvw, pw, d = dm.attn_width, dm.kv_width, dm.pool_width, dm.d_model
    k0, v0, u0 = aw, aw + kvw, aw + 2 * kvw
    return jnp.concatenate([_group_major(w[:, :aw], 1), w[:, u0:], w[:, k0:v0], w[:, v0:u0]], axis=1).astype(BF16)


def _moe_plan(dm, eidx, rank, counts):
    bm, e = dm.bm, dm.n_experts
    counts = counts.reshape(e)
    padded = (counts + bm - 1) // bm * bm
    pend = jnp.cumsum(padded)
    pstart = pend - padded
    onehot = eidx[None, :, :] == jnp.arange(e, dtype=I32)[:, None, None]
    dest = jnp.sum(jnp.where(onehot, pstart[:, None, None], 0), axis=0) + rank
    dest_tiles = dest.T.reshape(dm.t // dm.tm_moe, 1, dm.tm_moe * TOP_K)
    block_start = jnp.arange(dm.n_blocks, dtype=I32) * bm
    block_e = jnp.minimum(jnp.sum(block_start[:, None] >= pend[None, :], axis=1), e - 1).astype(I32)
    n_used = (pend[-1] // bm).astype(I32).reshape(1)
    blocks = jnp.arange(dm.n_blocks, dtype=I32)
    prev_e = jnp.concatenate([jnp.full((1,), -1, I32), block_e[:-1]])
    first = jnp.logical_and(blocks < n_used[0], block_e != prev_e).astype(I32)
    slot = (jnp.cumsum(first) - 1) % 2
    ids = jnp.arange(e, dtype=I32)
    at_or_after = lax.cummin(jnp.where(padded > 0, ids, e)[::-1])[::-1]
    after = jnp.concatenate([at_or_after[1:], jnp.full((1,), e, I32)])
    nxt = jnp.take(jnp.where(after < e, after, -1), block_e)
    runs = (block_e, n_used, first, slot.astype(I32), nxt.astype(I32))
    return pend.astype(I32), padded.astype(I32), dest_tiles.astype(I32), runs


def _forward(dm, x_prompt, x_sample, cache_k, cache_v, state_pool, ln_in_g, ln_in_b, w_in, sinks, w_pool,
             pool_scale, w_branch_a, w_branch_b, w_out, ln1_g, ln1_b, w_router, router_bias, w_gate_e, w_up_e,
             w_down_e, w_gate_s, w_up_s, w_down_s, ln2_g, ln2_b):
    d, aw, kvw, pw, e = dm.d_model, dm.attn_width, dm.kv_width, dm.pool_width, dm.n_experts
    row2 = lambda v: v.reshape(1, -1)
    tri = (jnp.arange(dm.tr_route)[:, None] < jnp.arange(dm.tr_route)[None, :]).astype(BF16)
    sink_heads = np.array([GQA_GROUP * h + g for g in range(GQA_GROUP) for h in range(N_KV_HEADS)])
    kcol0 = aw + pw + 2 * d

    xf, xb = _ln_in(dm, x_prompt.reshape(dm.tp, d), x_sample.reshape(dm.ts, d), row2(ln_in_g), row2(ln_in_b))
    outs = [[] for _ in range(6)]
    for l in range(dm.depth):
        z = _in_proj(dm, xb, _arrange_w_in(dm, w_in[l]))
        sink_col = jnp.broadcast_to(jnp.repeat(sinks[l][sink_heads], CHUNK).reshape(aw, 1), (aw, LANES))
        a_p = _attn_prompt(dm, z, sink_col)
        a_s = _attn_sample(dm, z, cache_k[l].reshape(dm.dec_batch * dm.window, kvw),
                           cache_v[l].reshape(dm.dec_batch * dm.window, kvw), sink_col)
        hist = jnp.pad(state_pool[l], ((0, 0), (1, 0), (0, 0))).reshape(dm.dec_batch * POOL_HIST_ROWS, pw)
        d_p = _pool_prompt(dm, z)
        d_s = _pool_sample(dm, z, hist)
        wr = jnp.pad(w_router[l], ((0, 0), (0, ROUTER_LANES - e))).astype(BF16)
        x1f, x1p, lg = _post(dm, a_p, a_s, d_p, d_s, z, xf, w_pool[l].astype(BF16), row2(pool_scale[l]),
                              _group_major(w_branch_a[l], 0).astype(BF16), w_branch_b[l].astype(BF16),
                              w_out[l].astype(BF16), row2(ln1_g[l]), row2(ln1_b[l]), wr)
        eidx, gwt, rank, counts = _route(dm, lg, router_bias[l].reshape(e, 1), tri)
        pend, padded, dest_tiles, runs = _moe_plan(dm, eidx, rank, counts)
        xs = _dispatch(dm, pend, padded, dest_tiles, x1p)
        ys = _experts(dm, l, runs, xs, w_gate_e, w_up_e, w_down_e)
        xf, xb = _combine(dm, dest_tiles, x1f, gwt.T, ys, w_gate_s[l].astype(BF16), w_up_s[l].astype(BF16),
                          w_down_s[l].astype(BF16), row2(ln2_g[l]), row2(ln2_b[l]))
        def prompt_tail(col0, width, n_rows):
            return jnp.stack([z[(b + 1) * dm.seq - n_rows:(b + 1) * dm.seq, col0:col0 + width]
                              for b in range(dm.batch)])

        def sample_cols(col0, width):
            return z[dm.tp:, col0:col0 + width].reshape(dm.dec_batch, dm.dec_seq, width)

        kv_shape = (N_KV_HEADS, HEAD_DIM)
        outs[0].append(prompt_tail(kcol0, kvw, dm.window).reshape(dm.batch, dm.window, *kv_shape))
        outs[1].append(prompt_tail(kcol0 + kvw, kvw, dm.window).reshape(dm.batch, dm.window, *kv_shape))
        outs[2].append(prompt_tail(aw, pw, 15))
        ks = sample_cols(kcol0, kvw).reshape(dm.dec_batch, dm.dec_seq, *kv_shape)
        vs = sample_cols(kcol0 + kvw, kvw).reshape(dm.dec_batch, dm.dec_seq, *kv_shape)
        outs[3].append(jnp.concatenate([cache_k[l], ks], axis=1)[:, -dm.window:])
        outs[4].append(jnp.concatenate([cache_v[l], vs], axis=1)[:, -dm.window:])
        outs[5].append(jnp.concatenate([state_pool[l], sample_cols(aw, pw)], axis=1)[:, -15:])
    y_prompt = xf[:dm.tp].reshape(dm.batch, dm.seq, d)
    y_sample = xf[dm.tp:].reshape(dm.dec_batch, dm.dec_seq, d)
    return (y_prompt, y_sample) + tuple(jnp.stack(o) for o in outs)


def kernel(x_prompt, x_sample, cache_k, cache_v, state_pool, ln_in_g, ln_in_b, w_in, sinks, w_pool, pool_scale,
           w_branch_a, w_branch_b, w_out, ln1_g, ln1_b, w_router, router_bias, w_gate_e, w_up_e, w_down_e,
           w_gate_s, w_up_s, w_down_s, ln2_g, ln2_b):
    dm = Dims()
    return _forward(dm, x_prompt, x_sample, cache_k, cache_v, state_pool, ln_in_g, ln_in_b, w_in, sinks, w_pool,
                    pool_scale, w_branch_a, w_branch_b, w_out, ln1_g, ln1_b, w_router, router_bias, w_gate_e,
                    w_up_e, w_down_e, w_gate_s, w_up_s, w_down_s, ln2_g, ln2_b)
```

```python
import dataclasses
import functools

import numpy as np
import jax
import jax.numpy as jnp
from jax import lax
from jax.experimental import pallas as pl
from jax.experimental.pallas import tpu as pltpu

F32 = jnp.float32
BF16 = jnp.bfloat16
U32 = jnp.uint32
I32 = jnp.int32

LN_EPS = 1e-5
HEAD_DIM = 64
CHUNK = 64
N_KV_HEADS = 4
GQA_GROUP = 4
WINDOW_CHUNKS = 2
POOL_WINDOWS = (2, 4, 8, 16)
POOL_HIST_ROWS = 16
N_EXPERT_GROUPS = 8
TOPK_GROUPS = 4
TOP_K = 8
ROUTED_SCALE = 2.5
LANES = 128
ROUTER_LANES = LANES
ATTN_KEY_PAD = 2 * LANES
V7X_VMEM_LIMIT_BYTES = 56 * 1024 * 1024


@dataclasses.dataclass(frozen=True)
class Dims:
    d_model: int = 2048
    batch: int = 8
    seq: int = 2048
    depth: int = 4
    dec_batch: int = 32
    dec_seq: int = 64
    past_len: int = 2048
    window: int = 128
    pool_width: int = 1024
    n_experts: int = 64
    expert_dim: int = 512
    shared_dim: int = 512
    tm_ln: int = 512
    tm_proj: int = 1024
    tn_proj: int = 512
    attn_chunks: int = 4
    tr_pool: int = 512
    tm_post: int = 256
    tr_route: int = 512
    tm_disp: int = 512
    tm_moe: int = 256
    bm: int = 256

    @property
    def attn_width(self):
        return N_KV_HEADS * GQA_GROUP * HEAD_DIM

    @property
    def kv_width(self):
        return N_KV_HEADS * HEAD_DIM

    @property
    def in_width(self):
        return self.attn_width + 2 * self.kv_width + self.pool_width + 2 * self.d_model

    @property
    def tp(self):
        return self.batch * self.seq

    @property
    def ts(self):
        return self.dec_batch * self.dec_seq

    @property
    def t(self):
        return self.tp + self.ts

    @property
    def row_lines(self):
        return self.d_model // 2 // LANES

    @property
    def n_blocks(self):
        return self.t * TOP_K // self.bm + self.n_experts

    @property
    def alpha(self):
        return (2 * self.depth) ** 0.25


def _cparams(sem):
    return pltpu.CompilerParams(dimension_semantics=sem, vmem_limit_bytes=V7X_VMEM_LIMIT_BYTES)


def _const_spec(shape):
    nd = len(shape)
    return pl.BlockSpec(shape, lambda *_: (0,) * nd, pipeline_mode=pl.Buffered(1))


def _layer_norm(h, g, b):
    mu = jnp.mean(h, axis=-1, keepdims=True)
    c = h - mu
    var = jnp.mean(c * c, axis=-1, keepdims=True)
    return c * lax.rsqrt(var + LN_EPS) * g + b


def _pack_bf16_pair(lo, hi):
    lo_bits = lax.bitcast_convert_type(lo.astype(BF16).astype(F32), U32)
    hi_bits = lax.bitcast_convert_type(hi.astype(BF16).astype(F32), U32)
    return (lo_bits >> 16) | hi_bits


def _store_row_tiles(ref, x):
    n_chunks = x.shape[1] // LANES
    for j in range(n_chunks):
        ref[pl.ds(j, x.shape[0], stride=n_chunks), :] = x[:, j * LANES:(j + 1) * LANES]


def _load_row_tile_chunk(ref, j, rows, n_chunks):
    return ref[pl.ds(j, rows, stride=n_chunks), :]


def _unpack_bf16_pair(p):
    lo = lax.bitcast_convert_type(p << 16, F32)
    hi = lax.bitcast_convert_type(p & jnp.uint32(0xFFFF0000), F32)
    return lo, hi


def _ln_in_body(xp_ref, xs_ref, g_ref, b_ref, xf_ref, xb_ref, *, n_prompt_tiles):
    i = pl.program_id(0)

    def emit(x):
        y = _layer_norm(x, g_ref[...], b_ref[...])
        xf_ref[...] = y
        xb_ref[...] = y.astype(BF16)

    @pl.when(i < n_prompt_tiles)
    def _():
        emit(xp_ref[...])

    @pl.when(i >= n_prompt_tiles)
    def _():
        emit(xs_ref[...])


def _ln_in(dm, xp, xs, g, b):
    tm, d = dm.tm_ln, dm.d_model
    npt, nst = dm.tp // tm, dm.ts // tm
    return pl.pallas_call(
        functools.partial(_ln_in_body, n_prompt_tiles=npt),
        name="ln_in",
        grid=(npt + nst,),
        in_specs=[
            pl.BlockSpec((tm, d), lambda i: (jnp.minimum(i, npt - 1), 0)),
            pl.BlockSpec((tm, d), lambda i: (jnp.maximum(i - npt, 0), 0)),
            _const_spec((1, d)),
            _const_spec((1, d)),
        ],
        out_specs=[pl.BlockSpec((tm, d), lambda i: (i, 0)), pl.BlockSpec((tm, d), lambda i: (i, 0))],
        out_shape=[jax.ShapeDtypeStruct((dm.t, d), F32), jax.ShapeDtypeStruct((dm.t, d), BF16)],
        compiler_params=_cparams(("arbitrary",)),
    )(xp, xs, g, b)


def _in_proj_body(x_ref, w_ref, z_ref):
    z_ref[...] = jnp.dot(x_ref[...], w_ref[...], preferred_element_type=F32)


def _in_proj(dm, xb, w):
    tm, tn, d, n = dm.tm_proj, dm.tn_proj, dm.d_model, dm.in_width
    return pl.pallas_call(
        _in_proj_body,
        name="in_proj",
        grid=(dm.t // tm, n // tn),
        in_specs=[pl.BlockSpec((tm, d), lambda i, j: (i, 0)), pl.BlockSpec((d, tn), lambda i, j: (0, j))],
        out_specs=pl.BlockSpec((tm, tn), lambda i, j: (i, j)),
        out_shape=jax.ShapeDtypeStruct((dm.t, n), F32),
        compiler_params=_cparams(("arbitrary", "arbitrary")),
    )(xb, w)


def _attn_chunk(q, k, v_ext, sink, n_missing):
    kvw = N_KV_HEADS * HEAD_DIM
    n_keys = (WINDOW_CHUNKS + 1) * CHUNK
    lane_head = lax.broadcasted_iota(I32, (CHUNK, kvw), 1) >> 6
    blocks = []
    for g in range(GQA_GROUP):
        slab = q[:, g * kvw:(g + 1) * kvw]
        for h in range(N_KV_HEADS):
            blocks.append(jnp.where(lane_head == h, slab, 0.0).astype(BF16))
    lhs = jnp.concatenate(blocks, axis=0)
    s = lax.dot_general(lhs, k, (((1,), (1,)), ((), ())), preferred_element_type=F32)
    col = lax.broadcasted_iota(I32, s.shape, 1)
    valid = col < n_keys if n_missing is None else jnp.logical_and(col < n_keys, col >= n_missing)
    s = jnp.where(valid, s, -jnp.inf)
    lanes = sink.shape[1]
    s_lo, s_hi = s[:, :lanes], s[:, lanes:]
    row_max = jnp.max(jnp.maximum(s_lo, s_hi), axis=-1, keepdims=True)
    m = jnp.maximum(jnp.broadcast_to(row_max, sink.shape), sink)
    p = jnp.concatenate([jnp.exp(s_lo - m), jnp.exp(s_hi - m)], axis=1).astype(BF16)
    o_ext = jnp.dot(p, v_ext, preferred_element_type=F32)
    denom = o_ext[:, kvw:] + jnp.exp(sink - m)
    inv = 1.0 / denom
    o = o_ext[:, :kvw] * jnp.concatenate([inv] * (kvw // lanes), axis=1)
    outs = []
    for g in range(GQA_GROUP):
        acc = jnp.zeros((CHUNK, kvw), F32)
        for h in range(N_KV_HEADS):
            rb = g * N_KV_HEADS + h
            acc = acc + jnp.where(lane_head == h, o[rb * CHUNK:(rb + 1) * CHUNK, :], 0.0)
        outs.append(acc.astype(BF16))
    return outs


def _attn_body(q_ref, kh_ref, kc_ref, vh_ref, vc_ref, sink_ref, o_ref, *, nch, banded):
    kvw = N_KV_HEADS * HEAD_DIM
    hist = WINDOW_CHUNKS * CHUNK
    pad = ATTN_KEY_PAD - hist - CHUNK
    sink = sink_ref[...]

    def with_ones(v):
        return jnp.concatenate([v, jnp.ones((v.shape[0], sink.shape[1]), BF16)], axis=1)

    kc = kc_ref[...].astype(BF16)
    vc = with_ones(vc_ref[...].astype(BF16))
    kh = kh_ref[...].astype(BF16)
    vh = with_ones(vh_ref[...].astype(BF16))
    kpad = jnp.zeros((pad, kvw), BF16)
    vpad = jnp.zeros((pad, vc.shape[1]), BF16)
    if banded:
        c0 = pl.program_id(1) * nch
        kall = jnp.concatenate([kh, kc, kpad], axis=0)
        vall = jnp.concatenate([vh, vc, vpad], axis=0)
    for j in range(nch):
        q = q_ref[j * CHUNK:(j + 1) * CHUNK, :] * (HEAD_DIM ** -0.5)
        if banded:
            k = kall[j * CHUNK:j * CHUNK + ATTN_KEY_PAD, :]
            v = vall[j * CHUNK:j * CHUNK + ATTN_KEY_PAD, :]
            n_missing = jnp.maximum(WINDOW_CHUNKS - c0 - j, 0) * CHUNK if j < WINDOW_CHUNKS else None
        else:
            k = jnp.concatenate([kh[j * hist:(j + 1) * hist, :], kc[j * CHUNK:(j + 1) * CHUNK, :], kpad], axis=0)
            v = jnp.concatenate([vh[j * hist:(j + 1) * hist, :], vc[j * CHUNK:(j + 1) * CHUNK, :], vpad], axis=0)
            n_missing = None
        outs = _attn_chunk(q, k, v, sink, n_missing)
        for g in range(GQA_GROUP):
            o_ref[j * CHUNK:(j + 1) * CHUNK, g * kvw:(g + 1) * kvw] = outs[g]


def _attn_prompt(dm, z, sink_col):
    nch = dm.attn_chunks
    rows, hist = nch * CHUNK, WINDOW_CHUNKS * CHUNK
    ns = dm.seq // rows
    aw, kvw = dm.attn_width, dm.kv_width
    kcol = (aw + dm.pool_width + 2 * dm.d_model) // kvw
    vcol = kcol + 1
    assert rows % hist == 0 and dm.seq % rows == 0

    def hist_spec(colblk):
        return pl.BlockSpec((hist, kvw), lambda b, i: (jnp.maximum((b * dm.seq + i * rows) // hist - 1, 0), colblk))

    def cur_spec(colblk):
        return pl.BlockSpec((rows, kvw), lambda b, i: (b * ns + i, colblk))

    return pl.pallas_call(
        functools.partial(_attn_body, nch=nch, banded=True),
        name="attn_prompt",
        grid=(dm.batch, ns),
        in_specs=[
            pl.BlockSpec((rows, aw), lambda b, i: (b * ns + i, 0)),
            hist_spec(kcol), cur_spec(kcol), hist_spec(vcol), cur_spec(vcol),
            _const_spec((aw, LANES)),
        ],
        out_specs=pl.BlockSpec((rows, aw), lambda b, i: (b * ns + i, 0)),
        out_shape=jax.ShapeDtypeStruct((dm.tp, aw), BF16),
        compiler_params=_cparams(("arbitrary", "arbitrary")),
    )(z, z, z, z, z, sink_col)


def _attn_sample(dm, z, ck, cv, sink_col):
    nch = dm.attn_chunks
    rows, hist = nch * CHUNK, WINDOW_CHUNKS * CHUNK
    aw, kvw = dm.attn_width, dm.kv_width
    kcol = (aw + dm.pool_width + 2 * dm.d_model) // kvw
    vcol = kcol + 1
    row0 = dm.tp // rows
    assert dm.dec_seq == CHUNK and dm.window == hist and dm.dec_batch % nch == 0 and dm.tp % rows == 0
    hist_spec = pl.BlockSpec((nch * hist, kvw), lambda b: (b, 0))

    def new_spec(colblk):
        return pl.BlockSpec((rows, kvw), lambda b: (row0 + b, colblk))

    return pl.pallas_call(
        functools.partial(_attn_body, nch=nch, banded=False),
        name="attn_sample",
        grid=(dm.dec_batch // nch,),
        in_specs=[
            pl.BlockSpec((rows, aw), lambda b: (row0 + b, 0)),
            hist_spec, new_spec(kcol), hist_spec, new_spec(vcol),
            _const_spec((aw, LANES)),
        ],
        out_specs=pl.BlockSpec((rows, aw), lambda b: (b, 0)),
        out_shape=jax.ShapeDtypeStruct((dm.ts, aw), BF16),
        compiler_params=_cparams(("arbitrary",)),
    )(z, ck, z, cv, z, sink_col)


def _pool_body(prev_ref, u_ref, o_ref, buf, *, tr, pos0, zero_first):
    i = pl.program_id(1)
    hr = POOL_HIST_ROWS
    prev = prev_ref[...]
    if zero_first:
        prev = jnp.where(i == 0, 0.0, prev)
    buf[0:hr, :] = prev
    buf[hr:hr + tr, :] = u_ref[...]
    pos = lax.broadcasted_iota(I32, (tr, 1), 0) + (pos0 + i * tr)
    gw = u_ref.shape[1] // len(POOL_WINDOWS)
    for g, w in enumerate(POOL_WINDOWS):
        cs = slice(g * gw, (g + 1) * gw)
        u = buf[hr:hr + tr, cs]
        acc = u
        for j in range(1, w):
            acc = acc + buf[hr - j:hr - j + tr, cs]
        cnt = jnp.minimum(pos + 1, w).astype(F32)
        o_ref[:, cs] = (acc / cnt - u).astype(BF16)


def _pool_prompt(dm, z):
    tr, pw, hr = dm.tr_pool, dm.pool_width, POOL_HIST_ROWS
    ucol = dm.attn_width // pw
    nt = dm.seq // tr
    return pl.pallas_call(
        functools.partial(_pool_body, tr=tr, pos0=0, zero_first=True),
        name="pool_prompt",
        grid=(dm.batch, nt),
        in_specs=[
            pl.BlockSpec((hr, pw), lambda b, i: (jnp.maximum((b * dm.seq + i * tr) // hr - 1, 0), ucol)),
            pl.BlockSpec((tr, pw), lambda b, i: (b * nt + i, ucol)),
        ],
        out_specs=pl.BlockSpec((tr, pw), lambda b, i: (b * nt + i, 0)),
        out_shape=jax.ShapeDtypeStruct((dm.tp, pw), BF16),
        scratch_shapes=[pltpu.VMEM((hr + tr, pw), F32)],
        compiler_params=_cparams(("arbitrary", "arbitrary")),
    )(z, z)


def _pool_sample(dm, z, hist):
    tr, pw, hr = dm.dec_seq, dm.pool_width, POOL_HIST_ROWS
    ucol = dm.attn_width // pw
    row0 = dm.tp // tr
    return pl.pallas_call(
        functools.partial(_pool_body, tr=tr, pos0=dm.past_len, zero_first=False),
        name="pool_sample",
        grid=(dm.dec_batch, 1),
        in_specs=[
            pl.BlockSpec((hr, pw), lambda b, i: (b, 0)),
            pl.BlockSpec((tr, pw), lambda b, i: (row0 + b, ucol)),
        ],
        out_specs=pl.BlockSpec((tr, pw), lambda b, i: (b, 0)),
        out_shape=jax.ShapeDtypeStruct((dm.ts, pw), BF16),
        scratch_shapes=[pltpu.VMEM((hr + tr, pw), F32)],
        compiler_params=_cparams(("arbitrary", "arbitrary")),
    )(hist, z)


def _post_body(ap_ref, as_ref, dp_ref, ds_ref, ga_ref, gb_ref, x_ref, wpool_ref, pscale_ref, wba_ref, wbb_ref,
               wout_ref, g_ref, b_ref, wr_ref, x1_ref, x1p_ref, lg_ref, *, alpha, n_prompt_tiles):
    is_prompt = pl.program_id(0) < n_prompt_tiles
    a = jnp.where(is_prompt, ap_ref[...], as_ref[...])
    d = jnp.where(is_prompt, dp_ref[...], ds_ref[...])
    n_groups = wpool_ref.shape[0]
    gw = d.shape[1] // n_groups
    pb = jnp.concatenate(
        [jnp.dot(d[:, g * gw:(g + 1) * gw], wpool_ref[g], preferred_element_type=F32) for g in range(n_groups)],
        axis=1)
    pb = (pb * pscale_ref[...]).astype(BF16)
    br_a = jnp.dot(a, wba_ref[...], preferred_element_type=F32)
    br_b = jnp.dot(pb, wbb_ref[...], preferred_element_type=F32)
    merged = jax.nn.sigmoid(ga_ref[...]) * br_a + jax.nn.sigmoid(gb_ref[...]) * br_b
    y = jnp.dot(merged.astype(BF16), wout_ref[...], preferred_element_type=F32)
    x1 = _layer_norm(alpha * x_ref[...] + y, g_ref[...], b_ref[...])
    x1_ref[...] = x1
    half = x1.shape[1] // 2
    _store_row_tiles(x1p_ref, _pack_bf16_pair(x1[:, :half], x1[:, half:]))
    lg_ref[...] = jnp.dot(x1.astype(BF16), wr_ref[...], preferred_element_type=F32)


def _post(dm, a_p, a_s, d_p, d_s, z, xf, wpool, pscale, wba, wbb, wout, g1, b1, wr):
    tm, dmod, aw, pw = dm.tm_post, dm.d_model, dm.attn_width, dm.pool_width
    gcol = (aw + pw) // dmod
    assert (aw + pw) % dmod == 0
    npt = dm.tp // tm
    row = lambda i: (i, 0)
    prow = lambda i: (jnp.minimum(i, npt - 1), 0)
    srow = lambda i: (jnp.maximum(i - npt, 0), 0)
    return pl.pallas_call(
        functools.partial(_post_body, alpha=dm.alpha, n_prompt_tiles=npt),
        name="post_mixer",
        grid=(dm.t // tm,),
        in_specs=[
            pl.BlockSpec((tm, aw), prow),
            pl.BlockSpec((tm, aw), srow),
            pl.BlockSpec((tm, pw), prow),
            pl.BlockSpec((tm, pw), srow),
            pl.BlockSpec((tm, dmod), lambda i: (i, gcol)),
            pl.BlockSpec((tm, dmod), lambda i: (i, gcol + 1)),
            pl.BlockSpec((tm, dmod), row),
            _const_spec(wpool.shape), _const_spec(pscale.shape), _const_spec(wba.shape),
            _const_spec(wbb.shape), _const_spec(wout.shape), _const_spec(g1.shape), _const_spec(b1.shape),
            _const_spec(wr.shape),
        ],
        out_specs=[
            pl.BlockSpec((tm, dmod), row),
            pl.BlockSpec((tm * dm.row_lines, LANES), row),
            pl.BlockSpec((tm, ROUTER_LANES), row),
        ],
        out_shape=[
            jax.ShapeDtypeStruct((dm.t, dmod), F32),
            jax.ShapeDtypeStruct((dm.t * dm.row_lines, LANES), U32),
            jax.ShapeDtypeStruct((dm.t, ROUTER_LANES), F32),
        ],
        compiler_params=_cparams(("arbitrary",)),
    )(a_p, a_s, d_p, d_s, z, z, xf, wpool, pscale, wba, wbb, wout, g1, b1, wr)


def _first_index_of_max(x, iota, n):
    mx = jnp.max(x, axis=0, keepdims=True)
    return jnp.min(jnp.where(x == mx, iota, n), axis=0, keepdims=True)


def _route_body(lg_ref, bias_ref, tri_ref, eidx_ref, gw_ref, rank_ref, cnt_ref, carry, *, n_experts):
    i = pl.program_id(0)

    @pl.when(i == 0)
    def _():
        carry[...] = jnp.zeros_like(carry)

    tr = lg_ref.shape[0]
    gsz = n_experts // N_EXPERT_GROUPS
    neg = -jnp.inf
    s = jax.nn.sigmoid(lg_ref[...].T[0:n_experts, :])
    sb = s + bias_ref[...]
    io_g = lax.broadcasted_iota(I32, (gsz, tr), 0).astype(F32)
    gs_rows = []
    for gi in range(N_EXPERT_GROUPS):
        blk = sb[gi * gsz:(gi + 1) * gsz, :]
        m1 = jnp.max(blk, axis=0, keepdims=True)
        first = _first_index_of_max(blk, io_g, gsz)
        m2 = jnp.max(jnp.where(io_g == first, neg, blk), axis=0, keepdims=True)
        gs_rows.append(m1 + m2)
    gs = jnp.concatenate(gs_rows, axis=0)
    io_ng = lax.broadcasted_iota(I32, (N_EXPERT_GROUPS, tr), 0).astype(F32)
    gsel = jnp.zeros((N_EXPERT_GROUPS, tr), F32)
    cur = gs
    for _ in range(TOPK_GROUPS):
        hit = io_ng == _first_index_of_max(cur, io_ng, N_EXPERT_GROUPS)
        gsel = jnp.where(hit, 1.0, gsel)
        cur = jnp.where(hit, neg, cur)
    emask = jnp.concatenate(
        [jnp.broadcast_to(gsel[gi:gi + 1, :], (gsz, tr)) for gi in range(N_EXPERT_GROUPS)], axis=0)
    cand = jnp.where(emask > 0.0, sb, neg)
    io_e = lax.broadcasted_iota(I32, (n_experts, tr), 0).astype(F32)
    sel = jnp.zeros((n_experts, tr), F32)
    idx_rows, w_rows = [], []
    for _ in range(TOP_K):
        idx = _first_index_of_max(cand, io_e, n_experts)
        hit = io_e == idx
        w_rows.append(jnp.sum(jnp.where(hit, s, 0.0), axis=0, keepdims=True))
        idx_rows.append(idx)
        sel = jnp.where(hit, 1.0, sel)
        cand = jnp.where(hit, neg, cand)
    w = jnp.concatenate(w_rows, axis=0)
    gw_ref[...] = w / jnp.sum(w, axis=0, keepdims=True) * ROUTED_SCALE
    eidx_ref[...] = jnp.concatenate(idx_rows, axis=0).astype(I32)
    pos = carry[...] + jnp.dot(sel.astype(BF16), tri_ref[...], preferred_element_type=F32)
    rank_rows = [jnp.sum(jnp.where(io_e == idx_rows[k], pos, 0.0), axis=0, keepdims=True) for k in range(TOP_K)]
    rank_ref[...] = jnp.concatenate(rank_rows, axis=0).astype(I32)
    total = carry[...] + jnp.sum(sel, axis=1, keepdims=True)
    carry[...] = total
    cnt_ref[...] = total.astype(I32)


def _route(dm, lg, bias_col, tri):
    tr, e = dm.tr_route, dm.n_experts
    blk = lambda i: (0, i)
    return pl.pallas_call(
        functools.partial(_route_body, n_experts=e),
        name="route",
        grid=(dm.t // tr,),
        in_specs=[pl.BlockSpec((tr, ROUTER_LANES), lambda i: (i, 0)), _const_spec((e, 1)), _const_spec((tr, tr))],
        out_specs=[pl.BlockSpec((TOP_K, tr), blk), pl.BlockSpec((TOP_K, tr), blk), pl.BlockSpec((TOP_K, tr), blk),
                   pl.BlockSpec((e, 1), lambda i: (0, 0))],
        out_shape=[jax.ShapeDtypeStruct((TOP_K, dm.t), I32), jax.ShapeDtypeStruct((TOP_K, dm.t), F32),
                   jax.ShapeDtypeStruct((TOP_K, dm.t), I32), jax.ShapeDtypeStruct((e, 1), I32)],
        scratch_shapes=[pltpu.VMEM((e, 1), F32)],
        compiler_params=_cparams(("arbitrary",)),
    )(lg, bias_col, tri)


def _dispatch_body(pend_ref, padded_ref, dest_ref, x_ref, x1_ref, wgs_ref, wus_ref, wds_ref, xs_ref, sh_ref,
                   zbuf, zsem, sem, *, bm, rl, n_experts):
    i = pl.program_id(0)
    tm = x_ref.shape[0] // rl
    n_rows = xs_ref.shape[0] // rl

    def zero_block(row_start):
        start = pl.multiple_of(row_start * rl, bm * rl)
        return pltpu.make_async_copy(zbuf, xs_ref.at[pl.ds(start, bm * rl)], zsem)

    @pl.when(i == 0)
    def _():
        zbuf[...] = jnp.zeros_like(zbuf)
        used_rows = pend_ref[n_experts - 1]

        def expert_tail(e, fn):
            @pl.when(padded_ref[e] > 0)
            def _():
                fn(zero_block(pend_ref[e] - bm))

        def unused_block(b, fn):
            fn(zero_block(used_rows + b * bm))

        n_unused = (n_rows - used_rows) // bm
        for fn in (lambda c: c.start(), lambda c: c.wait()):
            lax.fori_loop(0, n_experts, lambda e, carry, fn=fn: (expert_tail(e, fn), carry)[1], 0)
            lax.fori_loop(0, n_unused, lambda b, carry, fn=fn: (unused_block(b, fn), carry)[1], 0)

    def start_tok(t, carry):
        src = x_ref.at[pl.ds(pl.multiple_of(t * rl, rl), rl)]
        for k in range(TOP_K):
            dst = pl.multiple_of(dest_ref[0, 0, t * TOP_K + k] * rl, rl)
            pltpu.make_async_copy(src, xs_ref.at[pl.ds(dst, rl)], sem).start(
                priority=k % 2)
        return carry

    lax.fori_loop(0, tm, start_tok, 0)
    xb = x1_ref[...].astype(BF16)
    act = (jax.nn.silu(jnp.dot(xb, wgs_ref[...], preferred_element_type=F32))
           * jnp.dot(xb, wus_ref[...], preferred_element_type=F32))
    sh_ref[...] = jnp.dot(act.astype(BF16), wds_ref[...], preferred_element_type=F32)
    n = tm * TOP_K * rl
    pltpu.make_async_copy(xs_ref.at[pl.ds(0, n)], xs_ref.at[pl.ds(0, n)], sem).wait()


def _dispatch(dm, pend, padded, dest_tiles, x1p, x1f, wgs, wus, wds):
    tm, bm, rl, d = dm.tm_disp, dm.bm, dm.row_lines, dm.d_model
    n_rows = dm.n_blocks * bm
    dest_tiles = dest_tiles.reshape(dm.t // tm, 1, tm * TOP_K)

    def const(shape):
        nd = len(shape)
        return pl.BlockSpec(shape, lambda i, *_: (0,) * nd, pipeline_mode=pl.Buffered(1))

    return pl.pallas_call(
        functools.partial(_dispatch_body, bm=bm, rl=rl, n_experts=dm.n_experts),
        name="dispatch",
        grid_spec=pltpu.PrefetchScalarGridSpec(
            num_scalar_prefetch=2,
            grid=(dm.t // tm,),
            in_specs=[
                pl.BlockSpec((1, 1, tm * TOP_K), lambda i, *_: (i, 0, 0), memory_space=pltpu.SMEM),
                pl.BlockSpec((tm * rl, LANES), lambda i, *_: (i, 0)),
                pl.BlockSpec((tm, d), lambda i, *_: (i, 0)),
                const(wgs.shape), const(wus.shape), const(wds.shape),
            ],
            out_specs=[pl.BlockSpec(memory_space=pl.ANY), pl.BlockSpec((tm, d), lambda i, *_: (i, 0))],
            scratch_shapes=[pltpu.VMEM((bm * rl, LANES), U32), pltpu.SemaphoreType.DMA, pltpu.SemaphoreType.DMA],
        ),
        out_shape=[jax.ShapeDtypeStruct((n_rows * rl, LANES), U32), jax.ShapeDtypeStruct((dm.t, d), F32)],
        compiler_params=pltpu.CompilerParams(dimension_semantics=("arbitrary",), has_side_effects=True,
                                             vmem_limit_bytes=V7X_VMEM_LIMIT_BYTES),
    )(pend, padded, dest_tiles, x1p, x1f, wgs, wus, wds)


def _experts_body(be_ref, nu_ref, first_ref, slot_ref, nxt_ref, xs_ref, wg_hbm, wu_hbm, wd_hbm, ys_ref,
                  wg_f, wu_f, wd_f, wg_b, wu_b, wd_b, wsem, *, layer):
    b = pl.program_id(0)

    def fetch(e, slot):
        return [pltpu.make_async_copy(src.at[layer, e], dst.at[slot], wsem.at[slot])
                for src, dst in ((wg_hbm, wg_f), (wu_hbm, wu_f), (wd_hbm, wd_f))]

    @pl.when(b < nu_ref[0])
    def _():
        @pl.when(first_ref[b] == 1)
        def _():
            slot = slot_ref[b]

            @pl.when(b == 0)
            def _():
                for c in fetch(be_ref[0], slot):
                    c.start()

            @pl.when(nxt_ref[b] >= 0)
            def _():
                for c in fetch(nxt_ref[b], 1 - slot):
                    c.start()

            for c in fetch(be_ref[b], slot):
                c.wait()
            wg_b[...] = wg_f[slot].astype(BF16)
            wu_b[...] = wu_f[slot].astype(BF16)
            wd_b[...] = wd_f[slot].astype(BF16)

        rl = wg_b.shape[0] // 2 // LANES
        bm = xs_ref.shape[0] // rl
        xs = jnp.concatenate([_load_row_tile_chunk(xs_ref, j, bm, rl) for j in range(rl)], axis=1)
        lo, hi = _unpack_bf16_pair(xs)
        lo, hi = lo.astype(BF16), hi.astype(BF16)
        half = lo.shape[1]

        def proj(w_ref):
            return (jnp.dot(lo, w_ref[:half, :], preferred_element_type=F32)
                    + jnp.dot(hi, w_ref[half:, :], preferred_element_type=F32))

        act = jax.nn.silu(proj(wg_b)) * proj(wu_b)
        y = jnp.dot(act.astype(BF16), wd_b[...], preferred_element_type=F32)
        _store_row_tiles(ys_ref, _pack_bf16_pair(y[:, :half], y[:, half:]))

    @pl.when(b >= nu_ref[0])
    def _():
        ys_ref[...] = jnp.zeros_like(ys_ref)


def _experts(dm, layer, runs, xs, wg, wu, wd):
    bm, d, ed, rl = dm.bm, dm.d_model, dm.expert_dim, dm.row_lines
    any_spec = pl.BlockSpec(memory_space=pl.ANY)

    return pl.pallas_call(
        functools.partial(_experts_body, layer=layer),
        name="experts",
        grid_spec=pltpu.PrefetchScalarGridSpec(
            num_scalar_prefetch=5,
            grid=(dm.n_blocks,),
            in_specs=[
                pl.BlockSpec((bm * rl, LANES), lambda b, be, nu, *_: (jnp.minimum(b, nu[0] - 1), 0)),
                any_spec, any_spec, any_spec,
            ],
            out_specs=pl.BlockSpec((bm * rl, LANES), lambda b, *_: (b, 0)),
            scratch_shapes=[
                pltpu.VMEM((2, d, ed), F32), pltpu.VMEM((2, d, ed), F32), pltpu.VMEM((2, ed, d), F32),
                pltpu.VMEM((d, ed), BF16), pltpu.VMEM((d, ed), BF16), pltpu.VMEM((ed, d), BF16),
                pltpu.SemaphoreType.DMA((2,)),
            ],
        ),
        out_shape=jax.ShapeDtypeStruct((dm.n_blocks * bm * rl, LANES), U32),
        compiler_params=_cparams(("arbitrary",)),
    )(*runs, xs, wg, wu, wd)


def _combine_body(dest_ref, dnext_ref, x1_ref, sh_ref, gw_ref, ys_ref, g_ref, b_ref,
                  xf_ref, xb_ref, gbuf, sems, *, alpha):
    i = pl.program_id(0)
    n_steps = pl.num_programs(0)
    tm = x1_ref.shape[0]
    rl = gbuf.shape[2] // tm
    slot = i % 2

    def start_gathers(d_ref, to_slot):
        def start_tok(t, carry):
            land = pl.multiple_of(t * rl, rl)
            for k in range(TOP_K):
                src = pl.multiple_of(d_ref[0, 0, t * TOP_K + k] * rl, rl)
                pltpu.make_async_copy(ys_ref.at[pl.ds(src, rl)], gbuf.at[to_slot, k, pl.ds(land, rl)],
                                      sems.at[to_slot]).start(priority=k % 2)
            return carry

        lax.fori_loop(0, tm, start_tok, 0)

    @pl.when(i == 0)
    def _():
        start_gathers(dest_ref, slot)

    @pl.when(i + 1 < n_steps)
    def _():
        start_gathers(dnext_ref, 1 - slot)

    pltpu.make_async_copy(gbuf.at[slot], gbuf.at[slot], sems.at[slot]).wait()
    x1 = x1_ref[...]
    gw = gw_ref[...]
    wks = [jnp.broadcast_to(gw[:, k:k + 1], (tm, LANES)) for k in range(TOP_K)]
    r_lo, r_hi = [], []
    for j in range(rl):
        a_lo = jnp.zeros((tm, LANES), F32)
        a_hi = jnp.zeros((tm, LANES), F32)
        for k in range(TOP_K):
            lo, hi = _unpack_bf16_pair(_load_row_tile_chunk(gbuf.at[slot, k], j, tm, rl))
            a_lo = a_lo + lo * wks[k]
            a_hi = a_hi + hi * wks[k]
        r_lo.append(a_lo)
        r_hi.append(a_hi)
    f = sh_ref[...] + jnp.concatenate(r_lo + r_hi, axis=1)
    x2 = _layer_norm(alpha * x1 + f, g_ref[...], b_ref[...])
    xf_ref[...] = x2
    xb_ref[...] = x2.astype(BF16)


def _combine(dm, dest_tiles, x1f, shared, gw_tok, ys, g2, b2):
    tm, d, rl = dm.tm_moe, dm.d_model, dm.row_lines
    n_steps = dm.t // tm
    row = lambda i, *_: (i, 0)

    def const(shape):
        nd = len(shape)
        return pl.BlockSpec(shape, lambda i, *_: (0,) * nd, pipeline_mode=pl.Buffered(1))

    def dest_spec(ahead):
        return pl.BlockSpec((1, 1, tm * TOP_K), lambda i: (jnp.minimum(i + ahead, n_steps - 1), 0, 0),
                            memory_space=pltpu.SMEM)

    return pl.pallas_call(
        functools.partial(_combine_body, alpha=dm.alpha),
        name="combine",
        grid_spec=pltpu.PrefetchScalarGridSpec(
            num_scalar_prefetch=0,
            grid=(n_steps,),
            in_specs=[
                dest_spec(0), dest_spec(1),
                pl.BlockSpec((tm, d), row),
                pl.BlockSpec((tm, d), row),
                pl.BlockSpec((tm, TOP_K), row),
                pl.BlockSpec(memory_space=pl.ANY),
                const(g2.shape), const(b2.shape),
            ],
            out_specs=[pl.BlockSpec((tm, d), row), pl.BlockSpec((tm, d), row)],
            scratch_shapes=[pltpu.VMEM((2, TOP_K, tm * rl, LANES), U32), pltpu.SemaphoreType.DMA((2,))],
        ),
        out_shape=[jax.ShapeDtypeStruct((dm.t, d), F32), jax.ShapeDtypeStruct((dm.t, d), BF16)],
        compiler_params=_cparams(("arbitrary",)),
    )(dest_tiles, dest_tiles, x1f, shared, gw_tok, ys, g2, b2)


def _group_major(w, axis):
    shape = w.shape
    w = w.reshape(shape[:axis] + (N_KV_HEADS, GQA_GROUP, HEAD_DIM) + shape[axis + 1:])
    return jnp.swapaxes(w, axis, axis + 1).reshape(shape)


def _arrange_w_in(dm, w):
    aw, kvw, pw, d = dm.attn_width, dm.kv_width, dm.pool_width, dm.d_model
    k0, v0, u0 = aw, aw + kvw, aw + 2 * kvw
    return jnp.concatenate([_group_major(w[:, :aw], 1), w[:, u0:], w[:, k0:v0], w[:, v0:u0]], axis=1).astype(BF16)


def _moe_plan(dm, eidx, rank, counts):
    bm, e = dm.bm, dm.n_experts
    counts = counts.reshape(e)
    padded = (counts + bm - 1) // bm * bm
    pend = jnp.cumsum(padded)
    pstart = pend - padded
    onehot = eidx[None, :, :] == jnp.arange(e, dtype=I32)[:, None, None]
    dest = jnp.sum(jnp.where(onehot, pstart[:, None, None], 0), axis=0) + rank
    dest_tiles = dest.T.reshape(dm.t // dm.tm_moe, 1, dm.tm_moe * TOP_K)
    block_start = jnp.arange(dm.n_blocks, dtype=I32) * bm
    block_e = jnp.minimum(jnp.sum(block_start[:, None] >= pend[None, :], axis=1), e - 1).astype(I32)
    n_used = (pend[-1] // bm).astype(I32).reshape(1)
    blocks = jnp.arange(dm.n_blocks, dtype=I32)
    prev_e = jnp.concatenate([jnp.full((1,), -1, I32), block_e[:-1]])
    first = jnp.logical_and(blocks < n_used[0], block_e != prev_e).astype(I32)
    slot = (jnp.cumsum(first) - 1) % 2
    ids = jnp.arange(e, dtype=I32)
    at_or_after = lax.cummin(jnp.where(padded > 0, ids, e)[::-1])[::-1]
    after = jnp.concatenate([at_or_after[1:], jnp.full((1,), e, I32)])
    nxt = jnp.take(jnp.where(after < e, after, -1), block_e)
    runs = (block_e, n_used, first, slot.astype(I32), nxt.astype(I32))
    return pend.astype(I32), padded.astype(I32), dest_tiles.astype(I32), runs


def _forward(dm, x_prompt, x_sample, cache_k, cache_v, state_pool, ln_in_g, ln_in_b, w_in, sinks, w_pool,
             pool_scale, w_branch_a, w_branch_b, w_out, ln1_g, ln1_b, w_router, router_bias, w_gate_e, w_up_e,
             w_down_e, w_gate_s, w_up_s, w_down_s, ln2_g, ln2_b):
    d, aw, kvw, pw, e = dm.d_model, dm.attn_width, dm.kv_width, dm.pool_width, dm.n_experts
    row2 = lambda v: v.reshape(1, -1)
    tri = (jnp.arange(dm.tr_route)[:, None] < jnp.arange(dm.tr_route)[None, :]).astype(BF16)
    sink_heads = np.array([GQA_GROUP * h + g for g in range(GQA_GROUP) for h in range(N_KV_HEADS)])
    kcol0 = aw + pw + 2 * d

    xf, xb = _ln_in(dm, x_prompt.reshape(dm.tp, d), x_sample.reshape(dm.ts, d), row2(ln_in_g), row2(ln_in_b))
    outs = [[] for _ in range(6)]
    for l in range(dm.depth):
        z = _in_proj(dm, xb, _arrange_w_in(dm, w_in[l]))
        sink_col = jnp.broadcast_to(jnp.repeat(sinks[l][sink_heads], CHUNK).reshape(aw, 1), (aw, LANES))
        a_p = _attn_prompt(dm, z, sink_col)
        a_s = _attn_sample(dm, z, cache_k[l].reshape(dm.dec_batch * dm.window, kvw),
                           cache_v[l].reshape(dm.dec_batch * dm.window, kvw), sink_col)
        hist = jnp.pad(state_pool[l], ((0, 0), (1, 0), (0, 0))).reshape(dm.dec_batch * POOL_HIST_ROWS, pw)
        d_p = _pool_prompt(dm, z)
        d_s = _pool_sample(dm, z, hist)
        wr = jnp.pad(w_router[l], ((0, 0), (0, ROUTER_LANES - e))).astype(BF16)
        x1f, x1p, lg = _post(dm, a_p, a_s, d_p, d_s, z, xf, w_pool[l].astype(BF16), row2(pool_scale[l]),
                              _group_major(w_branch_a[l], 0).astype(BF16), w_branch_b[l].astype(BF16),
                              w_out[l].astype(BF16), row2(ln1_g[l]), row2(ln1_b[l]), wr)
        eidx, gwt, rank, counts = _route(dm, lg, router_bias[l].reshape(e, 1), tri)
        pend, padded, dest_tiles, runs = _moe_plan(dm, eidx, rank, counts)
        xs, shared = _dispatch(dm, pend, padded, dest_tiles, x1p, x1f, w_gate_s[l].astype(BF16),
                               w_up_s[l].astype(BF16), w_down_s[l].astype(BF16))
        ys = _experts(dm, l, runs, xs, w_gate_e, w_up_e, w_down_e)
        xf, xb = _combine(dm, dest_tiles, x1f, shared, gwt.T, ys, row2(ln2_g[l]), row2(ln2_b[l]))
        def prompt_tail(col0, width, n_rows):
            return jnp.stack([z[(b + 1) * dm.seq - n_rows:(b + 1) * dm.seq, col0:col0 + width]
                              for b in range(dm.batch)])

        def sample_cols(col0, width):
            return z[dm.tp:, col0:col0 + width].reshape(dm.dec_batch, dm.dec_seq, width)

        kv_shape = (N_KV_HEADS, HEAD_DIM)
        outs[0].append(prompt_tail(kcol0, kvw, dm.window).reshape(dm.batch, dm.window, *kv_shape))
        outs[1].append(prompt_tail(kcol0 + kvw, kvw, dm.window).reshape(dm.batch, dm.window, *kv_shape))
        outs[2].append(prompt_tail(aw, pw, 15))
        ks = sample_cols(kcol0, kvw).reshape(dm.dec_batch, dm.dec_seq, *kv_shape)
        vs = sample_cols(kcol0 + kvw, kvw).reshape(dm.dec_batch, dm.dec_seq, *kv_shape)
        outs[3].append(jnp.concatenate([cache_k[l], ks], axis=1)[:, -dm.window:])
        outs[4].append(jnp.concatenate([cache_v[l], vs], axis=1)[:, -dm.window:])
        outs[5].append(jnp.concatenate([state_pool[l], sample_cols(aw, pw)], axis=1)[:, -15:])
    y_prompt = xf[:dm.tp].reshape(dm.batch, dm.seq, d)
    y_sample = xf[dm.tp:].reshape(dm.dec_batch, dm.dec_seq, d)
    return (y_prompt, y_sample) + tuple(jnp.stack(o) for o in outs)


def kernel(x_prompt, x_sample, cache_k, cache_v, state_pool, ln_in_g, ln_in_b, w_in, sinks, w_pool, pool_scale,
           w_branch_a, w_branch_b, w_out, ln1_g, ln1_b, w_router, router_bias, w_gate_e, w_up_e, w_down_e,
           w_gate_s, w_up_s, w_down_s, ln2_g, ln2_b):
    dm = Dims()
    return _forward(dm, x_prompt, x_sample, cache_k, cache_v, state_pool, ln_in_g, ln_in_b, w_in, sinks, w_pool,
                    pool_scale, w_branch_a, w_branch_b, w_out, ln1_g, ln1_b, w_router, router_bias, w_gate_e,
                    w_up_e, w_down_e, w_gate_s, w_up_s, w_down_s, ln2_g, ln2_b)
```

```python
import dataclasses
import functools

import numpy as np
import jax
import jax.numpy as jnp
from jax import lax
from jax.experimental import pallas as pl
from jax.experimental.pallas import tpu as pltpu

F32 = jnp.float32
BF16 = jnp.bfloat16
U32 = jnp.uint32
I32 = jnp.int32

LN_EPS = 1e-5
HEAD_DIM = 64
CHUNK = 64
N_KV_HEADS = 4
GQA_GROUP = 4
WINDOW_CHUNKS = 2
POOL_WINDOWS = (2, 4, 8, 16)
POOL_HIST_ROWS = 16
N_EXPERT_GROUPS = 8
TOPK_GROUPS = 4
TOP_K = 8
ROUTED_SCALE = 2.5
LANES = 128
ROUTER_LANES = LANES
ATTN_KEY_PAD = 2 * LANES
V7X_VMEM_LIMIT_BYTES = 56 * 1024 * 1024


@dataclasses.dataclass(frozen=True)
class Dims:
    d_model: int = 2048
    batch: int = 8
    seq: int = 2048
    depth: int = 4
    dec_batch: int = 32
    dec_seq: int = 64
    past_len: int = 2048
    window: int = 128
    pool_width: int = 1024
    n_experts: int = 64
    expert_dim: int = 512
    shared_dim: int = 512
    tm_ln: int = 512
    tm_proj: int = 1024
    tn_proj: int = 512
    attn_chunks: int = 4
    tr_pool: int = 512
    tm_post: int = 256
    tr_route: int = 512
    tm_disp: int = 512
    tm_moe: int = 256
    bm: int = 256

    @property
    def attn_width(self):
        return N_KV_HEADS * GQA_GROUP * HEAD_DIM

    @property
    def kv_width(self):
        return N_KV_HEADS * HEAD_DIM

    @property
    def in_width(self):
        return self.attn_width + 2 * self.kv_width + self.pool_width + 2 * self.d_model

    @property
    def tp(self):
        return self.batch * self.seq

    @property
    def ts(self):
        return self.dec_batch * self.dec_seq

    @property
    def t(self):
        return self.tp + self.ts

    @property
    def row_lines(self):
        return self.d_model // 2 // LANES

    @property
    def n_blocks(self):
        return self.t * TOP_K // self.bm + self.n_experts

    @property
    def alpha(self):
        return (2 * self.depth) ** 0.25


def _cparams(sem):
    return pltpu.CompilerParams(dimension_semantics=sem, vmem_limit_bytes=V7X_VMEM_LIMIT_BYTES)


def _const_spec(shape):
    nd = len(shape)
    return pl.BlockSpec(shape, lambda *_: (0,) * nd, pipeline_mode=pl.Buffered(1))


def _layer_norm(h, g, b):
    mu = jnp.mean(h, axis=-1, keepdims=True)
    c = h - mu
    var = jnp.mean(c * c, axis=-1, keepdims=True)
    return c * lax.rsqrt(var + LN_EPS) * g + b


def _pack_bf16_pair(lo, hi):
    lo_bits = lax.bitcast_convert_type(lo.astype(BF16).astype(F32), U32)
    hi_bits = lax.bitcast_convert_type(hi.astype(BF16).astype(F32), U32)
    return (lo_bits >> 16) | hi_bits


def _store_row_tiles(ref, x):
    n_chunks = x.shape[1] // LANES
    for j in range(n_chunks):
        ref[pl.ds(j, x.shape[0], stride=n_chunks), :] = x[:, j * LANES:(j + 1) * LANES]


def _load_row_tile_chunk(ref, j, rows, n_chunks):
    return ref[pl.ds(j, rows, stride=n_chunks), :]


def _unpack_bf16_pair(p):
    lo = lax.bitcast_convert_type(p << 16, F32)
    hi = lax.bitcast_convert_type(p & jnp.uint32(0xFFFF0000), F32)
    return lo, hi


def _ln_in_body(xp_ref, xs_ref, g_ref, b_ref, xf_ref, xb_ref, *, n_prompt_tiles):
    i = pl.program_id(0)

    def emit(x):
        y = _layer_norm(x, g_ref[...], b_ref[...])
        xf_ref[...] = y
        xb_ref[...] = y.astype(BF16)

    @pl.when(i < n_prompt_tiles)
    def _():
        emit(xp_ref[...])

    @pl.when(i >= n_prompt_tiles)
    def _():
        emit(xs_ref[...])


def _ln_in(dm, xp, xs, g, b):
    tm, d = dm.tm_ln, dm.d_model
    npt, nst = dm.tp // tm, dm.ts // tm
    return pl.pallas_call(
        functools.partial(_ln_in_body, n_prompt_tiles=npt),
        name="ln_in",
        grid=(npt + nst,),
        in_specs=[
            pl.BlockSpec((tm, d), lambda i: (jnp.minimum(i, npt - 1), 0)),
            pl.BlockSpec((tm, d), lambda i: (jnp.maximum(i - npt, 0), 0)),
            _const_spec((1, d)),
            _const_spec((1, d)),
        ],
        out_specs=[pl.BlockSpec((tm, d), lambda i: (i, 0)), pl.BlockSpec((tm, d), lambda i: (i, 0))],
        out_shape=[jax.ShapeDtypeStruct((dm.t, d), F32), jax.ShapeDtypeStruct((dm.t, d), BF16)],
        compiler_params=_cparams(("arbitrary",)),
    )(xp, xs, g, b)


def _in_proj_body(x_ref, w_ref, z_ref):
    z_ref[...] = jnp.dot(x_ref[...], w_ref[...], preferred_element_type=F32)


def _in_proj(dm, xb, w):
    tm, tn, d, n = dm.tm_proj, dm.tn_proj, dm.d_model, dm.in_width
    return pl.pallas_call(
        _in_proj_body,
        name="in_proj",
        grid=(dm.t // tm, n // tn),
        in_specs=[pl.BlockSpec((tm, d), lambda i, j: (i, 0)), pl.BlockSpec((d, tn), lambda i, j: (0, j))],
        out_specs=pl.BlockSpec((tm, tn), lambda i, j: (i, j)),
        out_shape=jax.ShapeDtypeStruct((dm.t, n), F32),
        compiler_params=_cparams(("arbitrary", "arbitrary")),
    )(xb, w)


def _attn_chunk(q, k, v_ext, sink, n_missing):
    kvw = N_KV_HEADS * HEAD_DIM
    n_keys = (WINDOW_CHUNKS + 1) * CHUNK
    lane_head = lax.broadcasted_iota(I32, (CHUNK, kvw), 1) >> 6
    blocks = []
    for g in range(GQA_GROUP):
        slab = q[:, g * kvw:(g + 1) * kvw]
        for h in range(N_KV_HEADS):
            blocks.append(jnp.where(lane_head == h, slab, 0.0).astype(BF16))
    lhs = jnp.concatenate(blocks, axis=0)
    s = lax.dot_general(lhs, k, (((1,), (1,)), ((), ())), preferred_element_type=F32)
    col = lax.broadcasted_iota(I32, s.shape, 1)
    valid = col < n_keys if n_missing is None else jnp.logical_and(col < n_keys, col >= n_missing)
    s = jnp.where(valid, s, -jnp.inf)
    lanes = sink.shape[1]
    s_lo, s_hi = s[:, :lanes], s[:, lanes:]
    row_max = jnp.max(jnp.maximum(s_lo, s_hi), axis=-1, keepdims=True)
    m = jnp.maximum(jnp.broadcast_to(row_max, sink.shape), sink)
    p = jnp.concatenate([jnp.exp(s_lo - m), jnp.exp(s_hi - m)], axis=1).astype(BF16)
    o_ext = jnp.dot(p, v_ext, preferred_element_type=F32)
    denom = o_ext[:, kvw:] + jnp.exp(sink - m)
    inv = 1.0 / denom
    o = o_ext[:, :kvw] * jnp.concatenate([inv] * (kvw // lanes), axis=1)
    outs = []
    for g in range(GQA_GROUP):
        acc = jnp.zeros((CHUNK, kvw), F32)
        for h in range(N_KV_HEADS):
            rb = g * N_KV_HEADS + h
            acc = acc + jnp.where(lane_head == h, o[rb * CHUNK:(rb + 1) * CHUNK, :], 0.0)
        outs.append(acc.astype(BF16))
    return outs


def _attn_body(q_ref, kh_ref, kc_ref, vh_ref, vc_ref, sink_ref, o_ref, *, nch, banded):
    kvw = N_KV_HEADS * HEAD_DIM
    hist = WINDOW_CHUNKS * CHUNK
    pad = ATTN_KEY_PAD - hist - CHUNK
    sink = sink_ref[...]

    def with_ones(v):
        return jnp.concatenate([v, jnp.ones((v.shape[0], sink.shape[1]), BF16)], axis=1)

    kc = kc_ref[...].astype(BF16)
    vc = with_ones(vc_ref[...].astype(BF16))
    kh = kh_ref[...].astype(BF16)
    vh = with_ones(vh_ref[...].astype(BF16))
    kpad = jnp.zeros((pad, kvw), BF16)
    vpad = jnp.zeros((pad, vc.shape[1]), BF16)
    if banded:
        c0 = pl.program_id(1) * nch
        kall = jnp.concatenate([kh, kc, kpad], axis=0)
        vall = jnp.concatenate([vh, vc, vpad], axis=0)
    for j in range(nch):
        q = q_ref[j * CHUNK:(j + 1) * CHUNK, :] * (HEAD_DIM ** -0.5)
        if banded:
            k = kall[j * CHUNK:j * CHUNK + ATTN_KEY_PAD, :]
            v = vall[j * CHUNK:j * CHUNK + ATTN_KEY_PAD, :]
            n_missing = jnp.maximum(WINDOW_CHUNKS - c0 - j, 0) * CHUNK if j < WINDOW_CHUNKS else None
        else:
            k = jnp.concatenate([kh[j * hist:(j + 1) * hist, :], kc[j * CHUNK:(j + 1) * CHUNK, :], kpad], axis=0)
            v = jnp.concatenate([vh[j * hist:(j + 1) * hist, :], vc[j * CHUNK:(j + 1) * CHUNK, :], vpad], axis=0)
            n_missing = None
        outs = _attn_chunk(q, k, v, sink, n_missing)
        for g in range(GQA_GROUP):
            o_ref[j * CHUNK:(j + 1) * CHUNK, g * kvw:(g + 1) * kvw] = outs[g]


def _attn_prompt(dm, z, sink_col):
    nch = dm.attn_chunks
    rows, hist = nch * CHUNK, WINDOW_CHUNKS * CHUNK
    ns = dm.seq // rows
    aw, kvw = dm.attn_width, dm.kv_width
    kcol = (aw + dm.pool_width + 2 * dm.d_model) // kvw
    vcol = kcol + 1
    assert rows % hist == 0 and dm.seq % rows == 0

    def hist_spec(colblk):
        return pl.BlockSpec((hist, kvw), lambda b, i: (jnp.maximum((b * dm.seq + i * rows) // hist - 1, 0), colblk))

    def cur_spec(colblk):
        return pl.BlockSpec((rows, kvw), lambda b, i: (b * ns + i, colblk))

    return pl.pallas_call(
        functools.partial(_attn_body, nch=nch, banded=True),
        name="attn_prompt",
        grid=(dm.batch, ns),
        in_specs=[
            pl.BlockSpec((rows, aw), lambda b, i: (b * ns + i, 0)),
            hist_spec(kcol), cur_spec(kcol), hist_spec(vcol), cur_spec(vcol),
            _const_spec((aw, LANES)),
        ],
        out_specs=pl.BlockSpec((rows, aw), lambda b, i: (b * ns + i, 0)),
        out_shape=jax.ShapeDtypeStruct((dm.tp, aw), BF16),
        compiler_params=_cparams(("arbitrary", "arbitrary")),
    )(z, z, z, z, z, sink_col)


def _attn_sample(dm, z, ck, cv, sink_col):
    nch = dm.attn_chunks
    rows, hist = nch * CHUNK, WINDOW_CHUNKS * CHUNK
    aw, kvw = dm.attn_width, dm.kv_width
    kcol = (aw + dm.pool_width + 2 * dm.d_model) // kvw
    vcol = kcol + 1
    row0 = dm.tp // rows
    assert dm.dec_seq == CHUNK and dm.window == hist and dm.dec_batch % nch == 0 and dm.tp % rows == 0
    hist_spec = pl.BlockSpec((nch * hist, kvw), lambda b: (b, 0))

    def new_spec(colblk):
        return pl.BlockSpec((rows, kvw), lambda b: (row0 + b, colblk))

    return pl.pallas_call(
        functools.partial(_attn_body, nch=nch, banded=False),
        name="attn_sample",
        grid=(dm.dec_batch // nch,),
        in_specs=[
            pl.BlockSpec((rows, aw), lambda b: (row0 + b, 0)),
            hist_spec, new_spec(kcol), hist_spec, new_spec(vcol),
            _const_spec((aw, LANES)),
        ],
        out_specs=pl.BlockSpec((rows, aw), lambda b: (b, 0)),
        out_shape=jax.ShapeDtypeStruct((dm.ts, aw), BF16),
        compiler_params=_cparams(("arbitrary",)),
    )(z, ck, z, cv, z, sink_col)


def _pool_body(prev_ref, u_ref, o_ref, buf, *, tr, pos0, zero_first):
    i = pl.program_id(1)
    hr = POOL_HIST_ROWS
    prev = prev_ref[...]
    if zero_first:
        prev = jnp.where(i == 0, 0.0, prev)
    buf[0:hr, :] = prev
    buf[hr:hr + tr, :] = u_ref[...]
    pos = lax.broadcasted_iota(I32, (tr, 1), 0) + (pos0 + i * tr)
    gw = u_ref.shape[1] // len(POOL_WINDOWS)
    for g, w in enumerate(POOL_WINDOWS):
        cs = slice(g * gw, (g + 1) * gw)
        u = buf[hr:hr + tr, cs]
        acc = u
        for j in range(1, w):
            acc = acc + buf[hr - j:hr - j + tr, cs]
        cnt = jnp.minimum(pos + 1, w).astype(F32)
        o_ref[:, cs] = (acc / cnt - u).astype(BF16)


def _pool_prompt(dm, z):
    tr, pw, hr = dm.tr_pool, dm.pool_width, POOL_HIST_ROWS
    ucol = dm.attn_width // pw
    nt = dm.seq // tr
    return pl.pallas_call(
        functools.partial(_pool_body, tr=tr, pos0=0, zero_first=True),
        name="pool_prompt",
        grid=(dm.batch, nt),
        in_specs=[
            pl.BlockSpec((hr, pw), lambda b, i: (jnp.maximum((b * dm.seq + i * tr) // hr - 1, 0), ucol)),
            pl.BlockSpec((tr, pw), lambda b, i: (b * nt + i, ucol)),
        ],
        out_specs=pl.BlockSpec((tr, pw), lambda b, i: (b * nt + i, 0)),
        out_shape=jax.ShapeDtypeStruct((dm.tp, pw), BF16),
        scratch_shapes=[pltpu.VMEM((hr + tr, pw), F32)],
        compiler_params=_cparams(("arbitrary", "arbitrary")),
    )(z, z)


def _pool_sample(dm, z, hist):
    tr, pw, hr = dm.dec_seq, dm.pool_width, POOL_HIST_ROWS
    ucol = dm.attn_width // pw
    row0 = dm.tp // tr
    return pl.pallas_call(
        functools.partial(_pool_body, tr=tr, pos0=dm.past_len, zero_first=False),
        name="pool_sample",
        grid=(dm.dec_batch, 1),
        in_specs=[
            pl.BlockSpec((hr, pw), lambda b, i: (b, 0)),
            pl.BlockSpec((tr, pw), lambda b, i: (row0 + b, ucol)),
        ],
        out_specs=pl.BlockSpec((tr, pw), lambda b, i: (b, 0)),
        out_shape=jax.ShapeDtypeStruct((dm.ts, pw), BF16),
        scratch_shapes=[pltpu.VMEM((hr + tr, pw), F32)],
        compiler_params=_cparams(("arbitrary", "arbitrary")),
    )(hist, z)


def _post_body(ap_ref, as_ref, dp_ref, ds_ref, ga_ref, gb_ref, x_ref, wpool_ref, pscale_ref, wba_ref, wbb_ref,
               wout_ref, g_ref, b_ref, wr_ref, x1_ref, x1p_ref, lg_ref, *, alpha, n_prompt_tiles):
    is_prompt = pl.program_id(0) < n_prompt_tiles
    a = jnp.where(is_prompt, ap_ref[...], as_ref[...])
    d = jnp.where(is_prompt, dp_ref[...], ds_ref[...])
    n_groups = wpool_ref.shape[0]
    gw = d.shape[1] // n_groups
    pb = jnp.concatenate(
        [jnp.dot(d[:, g * gw:(g + 1) * gw], wpool_ref[g], preferred_element_type=F32) for g in range(n_groups)],
        axis=1)
    pb = (pb * pscale_ref[...]).astype(BF16)
    br_a = jnp.dot(a, wba_ref[...], preferred_element_type=F32)
    br_b = jnp.dot(pb, wbb_ref[...], preferred_element_type=F32)
    merged = jax.nn.sigmoid(ga_ref[...]) * br_a + jax.nn.sigmoid(gb_ref[...]) * br_b
    y = jnp.dot(merged.astype(BF16), wout_ref[...], preferred_element_type=F32)
    x1 = _layer_norm(alpha * x_ref[...] + y, g_ref[...], b_ref[...])
    x1_ref[...] = x1
    half = x1.shape[1] // 2
    _store_row_tiles(x1p_ref, _pack_bf16_pair(x1[:, :half], x1[:, half:]))
    lg_ref[...] = jnp.dot(x1.astype(BF16), wr_ref[...], preferred_element_type=F32)


def _post(dm, a_p, a_s, d_p, d_s, z, xf, wpool, pscale, wba, wbb, wout, g1, b1, wr):
    tm, dmod, aw, pw = dm.tm_post, dm.d_model, dm.attn_width, dm.pool_width
    gcol = (aw + pw) // dmod
    assert (aw + pw) % dmod == 0
    npt = dm.tp // tm
    row = lambda i: (i, 0)
    prow = lambda i: (jnp.minimum(i, npt - 1), 0)
    srow = lambda i: (jnp.maximum(i - npt, 0), 0)
    return pl.pallas_call(
        functools.partial(_post_body, alpha=dm.alpha, n_prompt_tiles=npt),
        name="post_mixer",
        grid=(dm.t // tm,),
        in_specs=[
            pl.BlockSpec((tm, aw), prow),
            pl.BlockSpec((tm, aw), srow),
            pl.BlockSpec((tm, pw), prow),
            pl.BlockSpec((tm, pw), srow),
            pl.BlockSpec((tm, dmod), lambda i: (i, gcol)),
            pl.BlockSpec((tm, dmod), lambda i: (i, gcol + 1)),
            pl.BlockSpec((tm, dmod), row),
            _const_spec(wpool.shape), _const_spec(pscale.shape), _const_spec(wba.shape),
            _const_spec(wbb.shape), _const_spec(wout.shape), _const_spec(g1.shape), _const_spec(b1.shape),
            _const_spec(wr.shape),
        ],
        out_specs=[
            pl.BlockSpec((tm, dmod), row),
            pl.BlockSpec((tm * dm.row_lines, LANES), row),
            pl.BlockSpec((tm, ROUTER_LANES), row),
        ],
        out_shape=[
            jax.ShapeDtypeStruct((dm.t, dmod), F32),
            jax.ShapeDtypeStruct((dm.t * dm.row_lines, LANES), U32),
            jax.ShapeDtypeStruct((dm.t, ROUTER_LANES), F32),
        ],
        compiler_params=_cparams(("arbitrary",)),
    )(a_p, a_s, d_p, d_s, z, z, xf, wpool, pscale, wba, wbb, wout, g1, b1, wr)


def _first_index_of_max(x, iota, n):
    mx = jnp.max(x, axis=0, keepdims=True)
    return jnp.min(jnp.where(x == mx, iota, n), axis=0, keepdims=True)


def _route_body(lg_ref, bias_ref, tri_ref, eidx_ref, gw_ref, rank_ref, cnt_ref, carry, *, n_experts):
    i = pl.program_id(0)

    @pl.when(i == 0)
    def _():
        carry[...] = jnp.zeros_like(carry)

    tr = lg_ref.shape[0]
    gsz = n_experts // N_EXPERT_GROUPS
    neg = -jnp.inf
    s = jax.nn.sigmoid(lg_ref[...].T[0:n_experts, :])
    sb = s + bias_ref[...]
    io_g = lax.broadcasted_iota(I32, (gsz, tr), 0).astype(F32)
    gs_rows = []
    for gi in range(N_EXPERT_GROUPS):
        blk = sb[gi * gsz:(gi + 1) * gsz, :]
        m1 = jnp.max(blk, axis=0, keepdims=True)
        first = _first_index_of_max(blk, io_g, gsz)
        m2 = jnp.max(jnp.where(io_g == first, neg, blk), axis=0, keepdims=True)
        gs_rows.append(m1 + m2)
    gs = jnp.concatenate(gs_rows, axis=0)
    io_ng = lax.broadcasted_iota(I32, (N_EXPERT_GROUPS, tr), 0).astype(F32)
    gsel = jnp.zeros((N_EXPERT_GROUPS, tr), F32)
    cur = gs
    for _ in range(TOPK_GROUPS):
        hit = io_ng == _first_index_of_max(cur, io_ng, N_EXPERT_GROUPS)
        gsel = jnp.where(hit, 1.0, gsel)
        cur = jnp.where(hit, neg, cur)
    emask = jnp.concatenate(
        [jnp.broadcast_to(gsel[gi:gi + 1, :], (gsz, tr)) for gi in range(N_EXPERT_GROUPS)], axis=0)
    cand = jnp.where(emask > 0.0, sb, neg)
    io_e = lax.broadcasted_iota(I32, (n_experts, tr), 0).astype(F32)
    sel = jnp.zeros((n_experts, tr), F32)
    idx_rows, w_rows = [], []
    for _ in range(TOP_K):
        idx = _first_index_of_max(cand, io_e, n_experts)
        hit = io_e == idx
        w_rows.append(jnp.sum(jnp.where(hit, s, 0.0), axis=0, keepdims=True))
        idx_rows.append(idx)
        sel = jnp.where(hit, 1.0, sel)
        cand = jnp.where(hit, neg, cand)
    w = jnp.concatenate(w_rows, axis=0)
    gw_ref[...] = w / jnp.sum(w, axis=0, keepdims=True) * ROUTED_SCALE
    eidx_ref[...] = jnp.concatenate(idx_rows, axis=0).astype(I32)
    pos = carry[...] + jnp.dot(sel.astype(BF16), tri_ref[...], preferred_element_type=F32)
    rank_rows = [jnp.sum(jnp.where(io_e == idx_rows[k], pos, 0.0), axis=0, keepdims=True) for k in range(TOP_K)]
    rank_ref[...] = jnp.concatenate(rank_rows, axis=0).astype(I32)
    total = carry[...] + jnp.sum(sel, axis=1, keepdims=True)
    carry[...] = total
    cnt_ref[...] = total.astype(I32)


def _route(dm, lg, bias_col, tri):
    tr, e = dm.tr_route, dm.n_experts
    blk = lambda i: (0, i)
    return pl.pallas_call(
        functools.partial(_route_body, n_experts=e),
        name="route",
        grid=(dm.t // tr,),
        in_specs=[pl.BlockSpec((tr, ROUTER_LANES), lambda i: (i, 0)), _const_spec((e, 1)), _const_spec((tr, tr))],
        out_specs=[pl.BlockSpec((TOP_K, tr), blk), pl.BlockSpec((TOP_K, tr), blk), pl.BlockSpec((TOP_K, tr), blk),
                   pl.BlockSpec((e, 1), lambda i: (0, 0))],
        out_shape=[jax.ShapeDtypeStruct((TOP_K, dm.t), I32), jax.ShapeDtypeStruct((TOP_K, dm.t), F32),
                   jax.ShapeDtypeStruct((TOP_K, dm.t), I32), jax.ShapeDtypeStruct((e, 1), I32)],
        scratch_shapes=[pltpu.VMEM((e, 1), F32)],
        compiler_params=_cparams(("arbitrary",)),
    )(lg, bias_col, tri)


def _dispatch_body(pend_ref, padded_ref, dest_ref, x_ref, xs_ref, zbuf, zsem, sem, *, bm, rl, n_experts):
    i = pl.program_id(0)
    tm = x_ref.shape[0] // rl
    n_rows = xs_ref.shape[0] // rl

    def zero_block(row_start):
        start = pl.multiple_of(row_start * rl, bm * rl)
        return pltpu.make_async_copy(zbuf, xs_ref.at[pl.ds(start, bm * rl)], zsem)

    @pl.when(i == 0)
    def _():
        zbuf[...] = jnp.zeros_like(zbuf)
        used_rows = pend_ref[n_experts - 1]

        def expert_tail(e, fn):
            @pl.when(padded_ref[e] > 0)
            def _():
                fn(zero_block(pend_ref[e] - bm))

        def unused_block(b, fn):
            fn(zero_block(used_rows + b * bm))

        n_unused = (n_rows - used_rows) // bm
        for fn in (lambda c: c.start(), lambda c: c.wait()):
            lax.fori_loop(0, n_experts, lambda e, carry, fn=fn: (expert_tail(e, fn), carry)[1], 0)
            lax.fori_loop(0, n_unused, lambda b, carry, fn=fn: (unused_block(b, fn), carry)[1], 0)

    def start_tok(t, carry):
        src = x_ref.at[pl.ds(pl.multiple_of(t * rl, rl), rl)]
        for k in range(TOP_K):
            dst = pl.multiple_of(dest_ref[0, 0, t * TOP_K + k] * rl, rl)
            pltpu.make_async_copy(src, xs_ref.at[pl.ds(dst, rl)], sem).start(
                priority=k % 2)
        return carry

    lax.fori_loop(0, tm, start_tok, 0)
    n = tm * TOP_K * rl
    pltpu.make_async_copy(xs_ref.at[pl.ds(0, n)], xs_ref.at[pl.ds(0, n)], sem).wait()


def _dispatch(dm, pend, padded, dest_tiles, x1p):
    tm, bm, rl = dm.tm_disp, dm.bm, dm.row_lines
    n_rows = dm.n_blocks * bm
    dest_tiles = dest_tiles.reshape(dm.t // tm, 1, tm * TOP_K)
    return pl.pallas_call(
        functools.partial(_dispatch_body, bm=bm, rl=rl, n_experts=dm.n_experts),
        name="dispatch",
        grid_spec=pltpu.PrefetchScalarGridSpec(
            num_scalar_prefetch=2,
            grid=(dm.t // tm,),
            in_specs=[
                pl.BlockSpec((1, 1, tm * TOP_K), lambda i, *_: (i, 0, 0), memory_space=pltpu.SMEM),
                pl.BlockSpec((tm * rl, LANES), lambda i, *_: (i, 0)),
            ],
            out_specs=pl.BlockSpec(memory_space=pl.ANY),
            scratch_shapes=[pltpu.VMEM((bm * rl, LANES), U32), pltpu.SemaphoreType.DMA, pltpu.SemaphoreType.DMA],
        ),
        out_shape=jax.ShapeDtypeStruct((n_rows * rl, LANES), U32),
        compiler_params=pltpu.CompilerParams(dimension_semantics=("arbitrary",), has_side_effects=True,
                                             vmem_limit_bytes=V7X_VMEM_LIMIT_BYTES),
    )(pend, padded, dest_tiles, x1p)


def _experts_body(be_ref, nu_ref, first_ref, slot_ref, nxt_ref, xs_ref, wg_hbm, wu_hbm, wd_hbm, ys_ref,
                  wg_f, wu_f, wd_f, wg_b, wu_b, wd_b, wsem, *, layer):
    b = pl.program_id(0)

    def fetch(e, slot):
        return [pltpu.make_async_copy(src.at[layer, e], dst.at[slot], wsem.at[slot])
                for src, dst in ((wg_hbm, wg_f), (wu_hbm, wu_f), (wd_hbm, wd_f))]

    @pl.when(b < nu_ref[0])
    def _():
        @pl.when(first_ref[b] == 1)
        def _():
            slot = slot_ref[b]

            @pl.when(b == 0)
            def _():
                for c in fetch(be_ref[0], slot):
                    c.start()

            @pl.when(nxt_ref[b] >= 0)
            def _():
                for c in fetch(nxt_ref[b], 1 - slot):
                    c.start()

            for c in fetch(be_ref[b], slot):
                c.wait()
            wg_b[...] = wg_f[slot].astype(BF16)
            wu_b[...] = wu_f[slot].astype(BF16)
            wd_b[...] = wd_f[slot].astype(BF16)

        rl = wg_b.shape[0] // 2 // LANES
        bm = xs_ref.shape[0] // rl
        xs = jnp.concatenate([_load_row_tile_chunk(xs_ref, j, bm, rl) for j in range(rl)], axis=1)
        lo, hi = _unpack_bf16_pair(xs)
        lo, hi = lo.astype(BF16), hi.astype(BF16)
        half = lo.shape[1]

        def proj(w_ref):
            return (jnp.dot(lo, w_ref[:half, :], preferred_element_type=F32)
                    + jnp.dot(hi, w_ref[half:, :], preferred_element_type=F32))

        act = jax.nn.silu(proj(wg_b)) * proj(wu_b)
        y = jnp.dot(act.astype(BF16), wd_b[...], preferred_element_type=F32)
        _store_row_tiles(ys_ref, _pack_bf16_pair(y[:, :half], y[:, half:]))

    @pl.when(b >= nu_ref[0])
    def _():
        ys_ref[...] = jnp.zeros_like(ys_ref)


def _experts(dm, layer, runs, xs, wg, wu, wd):
    bm, d, ed, rl = dm.bm, dm.d_model, dm.expert_dim, dm.row_lines
    any_spec = pl.BlockSpec(memory_space=pl.ANY)

    return pl.pallas_call(
        functools.partial(_experts_body, layer=layer),
        name="experts",
        grid_spec=pltpu.PrefetchScalarGridSpec(
            num_scalar_prefetch=5,
            grid=(dm.n_blocks,),
            in_specs=[
                pl.BlockSpec((bm * rl, LANES), lambda b, be, nu, *_: (jnp.minimum(b, nu[0] - 1), 0)),
                any_spec, any_spec, any_spec,
            ],
            out_specs=pl.BlockSpec((bm * rl, LANES), lambda b, *_: (b, 0)),
            scratch_shapes=[
                pltpu.VMEM((2, d, ed), F32), pltpu.VMEM((2, d, ed), F32), pltpu.VMEM((2, ed, d), F32),
                pltpu.VMEM((d, ed), BF16), pltpu.VMEM((d, ed), BF16), pltpu.VMEM((ed, d), BF16),
                pltpu.SemaphoreType.DMA((2,)),
            ],
        ),
        out_shape=jax.ShapeDtypeStruct((dm.n_blocks * bm * rl, LANES), U32),
        compiler_params=_cparams(("arbitrary",)),
    )(*runs, xs, wg, wu, wd)


def _combine_body(dest_ref, x1_ref, gw_ref, ys_ref, wgs_ref, wus_ref, wds_ref, g_ref, b_ref,
                  xf_ref, xb_ref, gbuf, sem, *, alpha, n_prompt_tiles):
    tm = x1_ref.shape[0]
    rl = gbuf.shape[1] // tm

    def start_tok(t, carry):
        land = pl.multiple_of(t * rl, rl)
        for k in range(TOP_K):
            src = pl.multiple_of(dest_ref[0, 0, t * TOP_K + k] * rl, rl)
            pltpu.make_async_copy(ys_ref.at[pl.ds(src, rl)], gbuf.at[k, pl.ds(land, rl)], sem).start(
                priority=k % 2)
        return carry

    lax.fori_loop(0, tm, start_tok, 0)
    x1 = x1_ref[...]
    xb = x1.astype(BF16)
    act = (jax.nn.silu(jnp.dot(xb, wgs_ref[...], preferred_element_type=F32))
           * jnp.dot(xb, wus_ref[...], preferred_element_type=F32))
    shared = jnp.dot(act.astype(BF16), wds_ref[...], preferred_element_type=F32)
    pltpu.make_async_copy(gbuf, gbuf, sem).wait()
    gw = gw_ref[...]
    wks = [jnp.broadcast_to(gw[:, k:k + 1], (tm, LANES)) for k in range(TOP_K)]
    r_lo, r_hi = [], []
    for j in range(rl):
        a_lo = jnp.zeros((tm, LANES), F32)
        a_hi = jnp.zeros((tm, LANES), F32)
        for k in range(TOP_K):
            lo, hi = _unpack_bf16_pair(_load_row_tile_chunk(gbuf.at[k], j, tm, rl))
            a_lo = a_lo + lo * wks[k]
            a_hi = a_hi + hi * wks[k]
        r_lo.append(a_lo)
        r_hi.append(a_hi)
    f = shared + jnp.concatenate(r_lo + r_hi, axis=1)
    x2 = _layer_norm(alpha * x1 + f, g_ref[...], b_ref[...])
    if n_prompt_tiles is None:
        xf_ref[...] = x2
        xb_ref[...] = x2.astype(BF16)
    else:
        is_prompt = pl.program_id(0) < n_prompt_tiles

        @pl.when(is_prompt)
        def _():
            xf_ref[...] = x2

        @pl.when(jnp.logical_not(is_prompt))
        def _():
            xb_ref[...] = x2


def _combine(dm, dest_tiles, x1f, gw_tok, ys, wgs, wus, wds, g2, b2, last):
    tm, d, rl = dm.tm_moe, dm.d_model, dm.row_lines
    row = lambda i, *_: (i, 0)
    npt = dm.tp // tm
    if last:
        out_specs = [pl.BlockSpec((tm, d), lambda i, *_: (jnp.minimum(i, npt - 1), 0)),
                     pl.BlockSpec((tm, d), lambda i, *_: (jnp.maximum(i - npt, 0), 0))]
        out_shape = [jax.ShapeDtypeStruct((dm.tp, d), F32), jax.ShapeDtypeStruct((dm.ts, d), F32)]
    else:
        out_specs = [pl.BlockSpec((tm, d), row), pl.BlockSpec((tm, d), row)]
        out_shape = [jax.ShapeDtypeStruct((dm.t, d), F32), jax.ShapeDtypeStruct((dm.t, d), BF16)]

    def const(shape):
        nd = len(shape)
        return pl.BlockSpec(shape, lambda i, *_: (0,) * nd, pipeline_mode=pl.Buffered(1))

    return pl.pallas_call(
        functools.partial(_combine_body, alpha=dm.alpha, n_prompt_tiles=npt if last else None),
        name="combine",
        grid_spec=pltpu.PrefetchScalarGridSpec(
            num_scalar_prefetch=0,
            grid=(dm.t // tm,),
            in_specs=[
                pl.BlockSpec((1, 1, tm * TOP_K), lambda i: (i, 0, 0), memory_space=pltpu.SMEM),
                pl.BlockSpec((tm, d), row),
                pl.BlockSpec((tm, TOP_K), row),
                pl.BlockSpec(memory_space=pl.ANY),
                const(wgs.shape), const(wus.shape), const(wds.shape), const(g2.shape), const(b2.shape),
            ],
            out_specs=out_specs,
            scratch_shapes=[pltpu.VMEM((TOP_K, tm * rl, LANES), U32), pltpu.SemaphoreType.DMA],
        ),
        out_shape=out_shape,
        compiler_params=_cparams(("arbitrary",)),
    )(dest_tiles, x1f, gw_tok, ys, wgs, wus, wds, g2, b2)


def _group_major(w, axis):
    shape = w.shape
    w = w.reshape(shape[:axis] + (N_KV_HEADS, GQA_GROUP, HEAD_DIM) + shape[axis + 1:])
    return jnp.swapaxes(w, axis, axis + 1).reshape(shape)


def _arrange_w_in(dm, w):
    aw, kvw, pw, d = dm.attn_width, dm.kv_width, dm.pool_width, dm.d_model
    k0, v0, u0 = aw, aw + kvw, aw + 2 * kvw
    return jnp.concatenate([_group_major(w[:, :aw], 1), w[:, u0:], w[:, k0:v0], w[:, v0:u0]], axis=1).astype(BF16)


def _moe_plan(dm, eidx, rank, counts):
    bm, e = dm.bm, dm.n_experts
    counts = counts.reshape(e)
    padded = (counts + bm - 1) // bm * bm
    pend = jnp.cumsum(padded)
    pstart = pend - padded
    onehot = eidx[None, :, :] == jnp.arange(e, dtype=I32)[:, None, None]
    dest = jnp.sum(jnp.where(onehot, pstart[:, None, None], 0), axis=0) + rank
    dest_tiles = dest.T.reshape(dm.t // dm.tm_moe, 1, dm.tm_moe * TOP_K)
    block_start = jnp.arange(dm.n_blocks, dtype=I32) * bm
    block_e = jnp.minimum(jnp.sum(block_start[:, None] >= pend[None, :], axis=1), e - 1).astype(I32)
    n_used = (pend[-1] // bm).astype(I32).reshape(1)
    blocks = jnp.arange(dm.n_blocks, dtype=I32)
    prev_e = jnp.concatenate([jnp.full((1,), -1, I32), block_e[:-1]])
    first = jnp.logical_and(blocks < n_used[0], block_e != prev_e).astype(I32)
    slot = (jnp.cumsum(first) - 1) % 2
    ids = jnp.arange(e, dtype=I32)
    at_or_after = lax.cummin(jnp.where(padded > 0, ids, e)[::-1])[::-1]
    after = jnp.concatenate([at_or_after[1:], jnp.full((1,), e, I32)])
    nxt = jnp.take(jnp.where(after < e, after, -1), block_e)
    runs = (block_e, n_used, first, slot.astype(I32), nxt.astype(I32))
    return pend.astype(I32), padded.astype(I32), dest_tiles.astype(I32), runs


def _forward(dm, x_prompt, x_sample, cache_k, cache_v, state_pool, ln_in_g, ln_in_b, w_in, sinks, w_pool,
             pool_scale, w_branch_a, w_branch_b, w_out, ln1_g, ln1_b, w_router, router_bias, w_gate_e, w_up_e,
             w_down_e, w_gate_s, w_up_s, w_down_s, ln2_g, ln2_b):
    d, aw, kvw, pw, e = dm.d_model, dm.attn_width, dm.kv_width, dm.pool_width, dm.n_experts
    row2 = lambda v: v.reshape(1, -1)
    tri = (jnp.arange(dm.tr_route)[:, None] < jnp.arange(dm.tr_route)[None, :]).astype(BF16)
    sink_heads = np.array([GQA_GROUP * h + g for g in range(GQA_GROUP) for h in range(N_KV_HEADS)])
    kcol0 = aw + pw + 2 * d

    xf, xb = _ln_in(dm, x_prompt.reshape(dm.tp, d), x_sample.reshape(dm.ts, d), row2(ln_in_g), row2(ln_in_b))
    outs = [[] for _ in range(6)]
    for l in range(dm.depth):
        z = _in_proj(dm, xb, _arrange_w_in(dm, w_in[l]))
        sink_col = jnp.broadcast_to(jnp.repeat(sinks[l][sink_heads], CHUNK).reshape(aw, 1), (aw, LANES))
        a_p = _attn_prompt(dm, z, sink_col)
        a_s = _attn_sample(dm, z, cache_k[l].reshape(dm.dec_batch * dm.window, kvw),
                           cache_v[l].reshape(dm.dec_batch * dm.window, kvw), sink_col)
        hist = jnp.pad(state_pool[l], ((0, 0), (1, 0), (0, 0))).reshape(dm.dec_batch * POOL_HIST_ROWS, pw)
        d_p = _pool_prompt(dm, z)
        d_s = _pool_sample(dm, z, hist)
        wr = jnp.pad(w_router[l], ((0, 0), (0, ROUTER_LANES - e))).astype(BF16)
        x1f, x1p, lg = _post(dm, a_p, a_s, d_p, d_s, z, xf, w_pool[l].astype(BF16), row2(pool_scale[l]),
                              _group_major(w_branch_a[l], 0).astype(BF16), w_branch_b[l].astype(BF16),
                              w_out[l].astype(BF16), row2(ln1_g[l]), row2(ln1_b[l]), wr)
        eidx, gwt, rank, counts = _route(dm, lg, router_bias[l].reshape(e, 1), tri)
        pend, padded, dest_tiles, runs = _moe_plan(dm, eidx, rank, counts)
        xs = _dispatch(dm, pend, padded, dest_tiles, x1p)
        ys = _experts(dm, l, runs, xs, w_gate_e, w_up_e, w_down_e)
        xf, xb = _combine(dm, dest_tiles, x1f, gwt.T, ys, w_gate_s[l].astype(BF16), w_up_s[l].astype(BF16),
                          w_down_s[l].astype(BF16), row2(ln2_g[l]), row2(ln2_b[l]), last=l == dm.depth - 1)
        def prompt_tail(col0, width, n_rows):
            if dm.t % dm.seq == 0:
                z_seq = z.reshape(dm.t // dm.seq, dm.seq, dm.in_width)
                return z_seq[:dm.batch, dm.seq - n_rows:, col0:col0 + width]
            return jnp.stack([z[(b + 1) * dm.seq - n_rows:(b + 1) * dm.seq, col0:col0 + width]
                              for b in range(dm.batch)])

        def sample_cols(col0, width):
            return z[dm.tp:, col0:col0 + width].reshape(dm.dec_batch, dm.dec_seq, width)

        outs[0].append(prompt_tail(kcol0, kvw, dm.window))
        outs[1].append(prompt_tail(kcol0 + kvw, kvw, dm.window))
        outs[2].append(prompt_tail(aw, pw, 15))
        outs[3].append(sample_cols(kcol0, kvw))
        outs[4].append(sample_cols(kcol0 + kvw, kvw))
        outs[5].append(sample_cols(aw, pw))
    y_prompt = xf.reshape(dm.batch, dm.seq, d)
    y_sample = xb.reshape(dm.dec_batch, dm.dec_seq, d)
    kp, vp, pp, ks, vs, ps = (jnp.stack(o) for o in outs)
    heads = (N_KV_HEADS, HEAD_DIM)
    new_k_prompt = kp.reshape(dm.depth, dm.batch, dm.window, *heads)
    new_v_prompt = vp.reshape(dm.depth, dm.batch, dm.window, *heads)
    ks = ks.reshape(dm.depth, dm.dec_batch, dm.dec_seq, *heads)
    vs = vs.reshape(dm.depth, dm.dec_batch, dm.dec_seq, *heads)
    new_k_sample = jnp.concatenate([cache_k, ks], axis=2)[:, :, -dm.window:]
    new_v_sample = jnp.concatenate([cache_v, vs], axis=2)[:, :, -dm.window:]
    new_pool_sample = jnp.concatenate([state_pool, ps], axis=2)[:, :, -15:]
    return (y_prompt, y_sample, new_k_prompt, new_v_prompt, pp, new_k_sample, new_v_sample, new_pool_sample)


def kernel(x_prompt, x_sample, cache_k, cache_v, state_pool, ln_in_g, ln_in_b, w_in, sinks, w_pool, pool_scale,
           w_branch_a, w_branch_b, w_out, ln1_g, ln1_b, w_router, router_bias, w_gate_e, w_up_e, w_down_e,
           w_gate_s, w_up_s, w_down_s, ln2_g, ln2_b):
    dm = Dims()
    return _forward(dm, x_prompt, x_sample, cache_k, cache_v, state_pool, ln_in_g, ln_in_b, w_in, sinks, w_pool,
                    pool_scale, w_branch_a, w_branch_b, w_out, ln1_g, ln1_b, w_router, router_bias, w_gate_e,
                    w_up_e, w_down_e, w_gate_s, w_up_s, w_down_s, ln2_g, ln2_b)
```

```python
import dataclasses
import functools

import numpy as np
import jax
import jax.numpy as jnp
from jax import lax
from jax.experimental import pallas as pl
from jax.experimental.pallas import tpu as pltpu

F32 = jnp.float32
BF16 = jnp.bfloat16
U32 = jnp.uint32
I32 = jnp.int32

LN_EPS = 1e-5
HEAD_DIM = 64
CHUNK = 64
N_KV_HEADS = 4
GQA_GROUP = 4
WINDOW_CHUNKS = 2
POOL_WINDOWS = (2, 4, 8, 16)
POOL_HIST_ROWS = 16
N_EXPERT_GROUPS = 8
TOPK_GROUPS = 4
TOP_K = 8
ROUTED_SCALE = 2.5
LANES = 128
ROUTER_LANES = LANES
ATTN_KEY_PAD = 2 * LANES
V7X_VMEM_LIMIT_BYTES = 56 * 1024 * 1024


@dataclasses.dataclass(frozen=True)
class Dims:
    d_model: int = 2048
    batch: int = 8
    seq: int = 2048
    depth: int = 4
    dec_batch: int = 32
    dec_seq: int = 64
    past_len: int = 2048
    window: int = 128
    pool_width: int = 1024
    n_experts: int = 64
    expert_dim: int = 512
    shared_dim: int = 512
    tm_ln: int = 512
    tm_proj: int = 1024
    tn_proj: int = 512
    attn_chunks: int = 4
    tr_pool: int = 512
    tm_post: int = 256
    tr_route: int = 512
    tr_dest: int = 2048
    tm_disp: int = 512
    tm_moe: int = 256
    bm: int = 256

    @property
    def attn_width(self):
        return N_KV_HEADS * GQA_GROUP * HEAD_DIM

    @property
    def kv_width(self):
        return N_KV_HEADS * HEAD_DIM

    @property
    def in_width(self):
        return self.attn_width + 2 * self.kv_width + self.pool_width + 2 * self.d_model

    @property
    def tp(self):
        return self.batch * self.seq

    @property
    def ts(self):
        return self.dec_batch * self.dec_seq

    @property
    def t(self):
        return self.tp + self.ts

    @property
    def row_lines(self):
        return self.d_model // 2 // LANES

    @property
    def n_blocks(self):
        return self.t * TOP_K // self.bm + self.n_experts

    @property
    def alpha(self):
        return (2 * self.depth) ** 0.25


def _cparams(sem):
    return pltpu.CompilerParams(dimension_semantics=sem, vmem_limit_bytes=V7X_VMEM_LIMIT_BYTES)


def _const_spec(shape):
    nd = len(shape)
    return pl.BlockSpec(shape, lambda *_: (0,) * nd, pipeline_mode=pl.Buffered(1))


def _layer_norm(h, g, b):
    mu = jnp.mean(h, axis=-1, keepdims=True)
    c = h - mu
    var = jnp.mean(c * c, axis=-1, keepdims=True)
    return c * lax.rsqrt(var + LN_EPS) * g + b


def _pack_bf16_pair(lo, hi):
    lo_bits = lax.bitcast_convert_type(lo.astype(BF16).astype(F32), U32)
    hi_bits = lax.bitcast_convert_type(hi.astype(BF16).astype(F32), U32)
    return (lo_bits >> 16) | hi_bits


def _store_row_tiles(ref, x):
    n_chunks = x.shape[1] // LANES
    for j in range(n_chunks):
        ref[pl.ds(j, x.shape[0], stride=n_chunks), :] = x[:, j * LANES:(j + 1) * LANES]


def _load_row_tile_chunk(ref, j, rows, n_chunks):
    return ref[pl.ds(j, rows, stride=n_chunks), :]


def _unpack_bf16_pair(p):
    lo = lax.bitcast_convert_type(p << 16, F32)
    hi = lax.bitcast_convert_type(p & jnp.uint32(0xFFFF0000), F32)
    return lo, hi


def _ln_in_body(xp_ref, xs_ref, g_ref, b_ref, xf_ref, xb_ref, *, n_prompt_tiles):
    i = pl.program_id(0)

    def emit(x):
        y = _layer_norm(x, g_ref[...], b_ref[...])
        xf_ref[...] = y
        xb_ref[...] = y.astype(BF16)

    @pl.when(i < n_prompt_tiles)
    def _():
        emit(xp_ref[...])

    @pl.when(i >= n_prompt_tiles)
    def _():
        emit(xs_ref[...])


def _ln_in(dm, xp, xs, g, b):
    tm, d = dm.tm_ln, dm.d_model
    npt, nst = dm.tp // tm, dm.ts // tm
    return pl.pallas_call(
        functools.partial(_ln_in_body, n_prompt_tiles=npt),
        name="ln_in",
        grid=(npt + nst,),
        in_specs=[
            pl.BlockSpec((tm, d), lambda i: (jnp.minimum(i, npt - 1), 0)),
            pl.BlockSpec((tm, d), lambda i: (jnp.maximum(i - npt, 0), 0)),
            _const_spec((1, d)),
            _const_spec((1, d)),
        ],
        out_specs=[pl.BlockSpec((tm, d), lambda i: (i, 0)), pl.BlockSpec((tm, d), lambda i: (i, 0))],
        out_shape=[jax.ShapeDtypeStruct((dm.t, d), F32), jax.ShapeDtypeStruct((dm.t, d), BF16)],
        compiler_params=_cparams(("arbitrary",)),
    )(xp, xs, g, b)


def _in_proj_body(x_ref, w_ref, z_ref):
    z_ref[...] = jnp.dot(x_ref[...], w_ref[...], preferred_element_type=F32)


def _in_proj(dm, xb, w):
    tm, tn, d, n = dm.tm_proj, dm.tn_proj, dm.d_model, dm.in_width
    return pl.pallas_call(
        _in_proj_body,
        name="in_proj",
        grid=(dm.t // tm, n // tn),
        in_specs=[pl.BlockSpec((tm, d), lambda i, j: (i, 0)), pl.BlockSpec((d, tn), lambda i, j: (0, j))],
        out_specs=pl.BlockSpec((tm, tn), lambda i, j: (i, j)),
        out_shape=jax.ShapeDtypeStruct((dm.t, n), F32),
        compiler_params=_cparams(("arbitrary", "arbitrary")),
    )(xb, w)


def _attn_chunk(q, k, v_ext, sink, n_missing):
    kvw = N_KV_HEADS * HEAD_DIM
    n_keys = (WINDOW_CHUNKS + 1) * CHUNK
    lane_head = lax.broadcasted_iota(I32, (CHUNK, kvw), 1) >> 6
    blocks = []
    for g in range(GQA_GROUP):
        slab = q[:, g * kvw:(g + 1) * kvw]
        for h in range(N_KV_HEADS):
            blocks.append(jnp.where(lane_head == h, slab, 0.0).astype(BF16))
    lhs = jnp.concatenate(blocks, axis=0)
    s = lax.dot_general(lhs, k, (((1,), (1,)), ((), ())), preferred_element_type=F32)
    col = lax.broadcasted_iota(I32, s.shape, 1)
    valid = col < n_keys if n_missing is None else jnp.logical_and(col < n_keys, col >= n_missing)
    s = jnp.where(valid, s, -jnp.inf)
    lanes = sink.shape[1]
    s_lo, s_hi = s[:, :lanes], s[:, lanes:]
    row_max = jnp.max(jnp.maximum(s_lo, s_hi), axis=-1, keepdims=True)
    m = jnp.maximum(jnp.broadcast_to(row_max, sink.shape), sink)
    p = jnp.concatenate([jnp.exp(s_lo - m), jnp.exp(s_hi - m)], axis=1).astype(BF16)
    o_ext = jnp.dot(p, v_ext, preferred_element_type=F32)
    denom = o_ext[:, kvw:] + jnp.exp(sink - m)
    inv = 1.0 / denom
    o = o_ext[:, :kvw] * jnp.concatenate([inv] * (kvw // lanes), axis=1)
    outs = []
    for g in range(GQA_GROUP):
        acc = jnp.zeros((CHUNK, kvw), F32)
        for h in range(N_KV_HEADS):
            rb = g * N_KV_HEADS + h
            acc = acc + jnp.where(lane_head == h, o[rb * CHUNK:(rb + 1) * CHUNK, :], 0.0)
        outs.append(acc.astype(BF16))
    return outs


def _attn_body(q_ref, kh_ref, kc_ref, vh_ref, vc_ref, sink_ref, o_ref, *, nch, banded):
    kvw = N_KV_HEADS * HEAD_DIM
    hist = WINDOW_CHUNKS * CHUNK
    pad = ATTN_KEY_PAD - hist - CHUNK
    sink = sink_ref[...]

    def with_ones(v):
        return jnp.concatenate([v, jnp.ones((v.shape[0], sink.shape[1]), BF16)], axis=1)

    kc = kc_ref[...].astype(BF16)
    vc = with_ones(vc_ref[...].astype(BF16))
    kh = kh_ref[...].astype(BF16)
    vh = with_ones(vh_ref[...].astype(BF16))
    kpad = jnp.zeros((pad, kvw), BF16)
    vpad = jnp.zeros((pad, vc.shape[1]), BF16)
    if banded:
        c0 = pl.program_id(1) * nch
        kall = jnp.concatenate([kh, kc, kpad], axis=0)
        vall = jnp.concatenate([vh, vc, vpad], axis=0)
    for j in range(nch):
        q = q_ref[j * CHUNK:(j + 1) * CHUNK, :] * (HEAD_DIM ** -0.5)
        if banded:
            k = kall[j * CHUNK:j * CHUNK + ATTN_KEY_PAD, :]
            v = vall[j * CHUNK:j * CHUNK + ATTN_KEY_PAD, :]
            n_missing = jnp.maximum(WINDOW_CHUNKS - c0 - j, 0) * CHUNK if j < WINDOW_CHUNKS else None
        else:
            k = jnp.concatenate([kh[j * hist:(j + 1) * hist, :], kc[j * CHUNK:(j + 1) * CHUNK, :], kpad], axis=0)
            v = jnp.concatenate([vh[j * hist:(j + 1) * hist, :], vc[j * CHUNK:(j + 1) * CHUNK, :], vpad], axis=0)
            n_missing = None
        outs = _attn_chunk(q, k, v, sink, n_missing)
        for g in range(GQA_GROUP):
            o_ref[j * CHUNK:(j + 1) * CHUNK, g * kvw:(g + 1) * kvw] = outs[g]


def _attn_prompt(dm, z, sink_col):
    nch = dm.attn_chunks
    rows, hist = nch * CHUNK, WINDOW_CHUNKS * CHUNK
    ns = dm.seq // rows
    aw, kvw = dm.attn_width, dm.kv_width
    kcol = (aw + dm.pool_width + 2 * dm.d_model) // kvw
    vcol = kcol + 1
    assert rows % hist == 0 and dm.seq % rows == 0

    def hist_spec(colblk):
        return pl.BlockSpec((hist, kvw), lambda b, i: (jnp.maximum((b * dm.seq + i * rows) // hist - 1, 0), colblk))

    def cur_spec(colblk):
        return pl.BlockSpec((rows, kvw), lambda b, i: (b * ns + i, colblk))

    return pl.pallas_call(
        functools.partial(_attn_body, nch=nch, banded=True),
        name="attn_prompt",
        grid=(dm.batch, ns),
        in_specs=[
            pl.BlockSpec((rows, aw), lambda b, i: (b * ns + i, 0)),
            hist_spec(kcol), cur_spec(kcol), hist_spec(vcol), cur_spec(vcol),
            _const_spec((aw, LANES)),
        ],
        out_specs=pl.BlockSpec((rows, aw), lambda b, i: (b * ns + i, 0)),
        out_shape=jax.ShapeDtypeStruct((dm.tp, aw), BF16),
        compiler_params=_cparams(("arbitrary", "arbitrary")),
    )(z, z, z, z, z, sink_col)


def _attn_sample(dm, z, ck, cv, sink_col):
    nch = dm.attn_chunks
    rows, hist = nch * CHUNK, WINDOW_CHUNKS * CHUNK
    aw, kvw = dm.attn_width, dm.kv_width
    kcol = (aw + dm.pool_width + 2 * dm.d_model) // kvw
    vcol = kcol + 1
    row0 = dm.tp // rows
    assert dm.dec_seq == CHUNK and dm.window == hist and dm.dec_batch % nch == 0 and dm.tp % rows == 0
    hist_spec = pl.BlockSpec((nch * hist, kvw), lambda b: (b, 0))

    def new_spec(colblk):
        return pl.BlockSpec((rows, kvw), lambda b: (row0 + b, colblk))

    return pl.pallas_call(
        functools.partial(_attn_body, nch=nch, banded=False),
        name="attn_sample",
        grid=(dm.dec_batch // nch,),
        in_specs=[
            pl.BlockSpec((rows, aw), lambda b: (row0 + b, 0)),
            hist_spec, new_spec(kcol), hist_spec, new_spec(vcol),
            _const_spec((aw, LANES)),
        ],
        out_specs=pl.BlockSpec((rows, aw), lambda b: (b, 0)),
        out_shape=jax.ShapeDtypeStruct((dm.ts, aw), BF16),
        compiler_params=_cparams(("arbitrary",)),
    )(z, ck, z, cv, z, sink_col)


def _pool_body(prev_ref, u_ref, o_ref, buf, *, tr, pos0, zero_first):
    i = pl.program_id(1)
    hr = POOL_HIST_ROWS
    prev = prev_ref[...]
    if zero_first:
        prev = jnp.where(i == 0, 0.0, prev)
    buf[0:hr, :] = prev
    buf[hr:hr + tr, :] = u_ref[...]
    pos = lax.broadcasted_iota(I32, (tr, 1), 0) + (pos0 + i * tr)
    gw = u_ref.shape[1] // len(POOL_WINDOWS)
    for g, w in enumerate(POOL_WINDOWS):
        cs = slice(g * gw, (g + 1) * gw)
        u = buf[hr:hr + tr, cs]
        acc = u
        for j in range(1, w):
            acc = acc + buf[hr - j:hr - j + tr, cs]
        cnt = jnp.minimum(pos + 1, w).astype(F32)
        o_ref[:, cs] = (acc / cnt - u).astype(BF16)


def _pool_prompt(dm, z):
    tr, pw, hr = dm.tr_pool, dm.pool_width, POOL_HIST_ROWS
    ucol = dm.attn_width // pw
    nt = dm.seq // tr
    return pl.pallas_call(
        functools.partial(_pool_body, tr=tr, pos0=0, zero_first=True),
        name="pool_prompt",
        grid=(dm.batch, nt),
        in_specs=[
            pl.BlockSpec((hr, pw), lambda b, i: (jnp.maximum((b * dm.seq + i * tr) // hr - 1, 0), ucol)),
            pl.BlockSpec((tr, pw), lambda b, i: (b * nt + i, ucol)),
        ],
        out_specs=pl.BlockSpec((tr, pw), lambda b, i: (b * nt + i, 0)),
        out_shape=jax.ShapeDtypeStruct((dm.tp, pw), BF16),
        scratch_shapes=[pltpu.VMEM((hr + tr, pw), F32)],
        compiler_params=_cparams(("arbitrary", "arbitrary")),
    )(z, z)


def _pool_sample(dm, z, hist):
    tr, pw, hr = dm.dec_seq, dm.pool_width, POOL_HIST_ROWS
    ucol = dm.attn_width // pw
    row0 = dm.tp // tr
    return pl.pallas_call(
        functools.partial(_pool_body, tr=tr, pos0=dm.past_len, zero_first=False),
        name="pool_sample",
        grid=(dm.dec_batch, 1),
        in_specs=[
            pl.BlockSpec((hr, pw), lambda b, i: (b, 0)),
            pl.BlockSpec((tr, pw), lambda b, i: (row0 + b, ucol)),
        ],
        out_specs=pl.BlockSpec((tr, pw), lambda b, i: (b, 0)),
        out_shape=jax.ShapeDtypeStruct((dm.ts, pw), BF16),
        scratch_shapes=[pltpu.VMEM((hr + tr, pw), F32)],
        compiler_params=_cparams(("arbitrary", "arbitrary")),
    )(hist, z)


def _post_body(ap_ref, as_ref, dp_ref, ds_ref, ga_ref, gb_ref, x_ref, wpool_ref, pscale_ref, wba_ref, wbb_ref,
               wout_ref, g_ref, b_ref, wr_ref, x1_ref, x1p_ref, lg_ref, *, alpha, n_prompt_tiles):
    is_prompt = pl.program_id(0) < n_prompt_tiles
    a = jnp.where(is_prompt, ap_ref[...], as_ref[...])
    d = jnp.where(is_prompt, dp_ref[...], ds_ref[...])
    n_groups = wpool_ref.shape[0]
    gw = d.shape[1] // n_groups
    pb = jnp.concatenate(
        [jnp.dot(d[:, g * gw:(g + 1) * gw], wpool_ref[g], preferred_element_type=F32) for g in range(n_groups)],
        axis=1)
    pb = (pb * pscale_ref[...]).astype(BF16)
    br_a = jnp.dot(a, wba_ref[...], preferred_element_type=F32)
    br_b = jnp.dot(pb, wbb_ref[...], preferred_element_type=F32)
    merged = jax.nn.sigmoid(ga_ref[...]) * br_a + jax.nn.sigmoid(gb_ref[...]) * br_b
    y = jnp.dot(merged.astype(BF16), wout_ref[...], preferred_element_type=F32)
    x1 = _layer_norm(alpha * x_ref[...] + y, g_ref[...], b_ref[...])
    x1_ref[...] = x1
    half = x1.shape[1] // 2
    _store_row_tiles(x1p_ref, _pack_bf16_pair(x1[:, :half], x1[:, half:]))
    lg_ref[...] = jnp.dot(x1.astype(BF16), wr_ref[...], preferred_element_type=F32)


def _post(dm, a_p, a_s, d_p, d_s, z, xf, wpool, pscale, wba, wbb, wout, g1, b1, wr):
    tm, dmod, aw, pw = dm.tm_post, dm.d_model, dm.attn_width, dm.pool_width
    gcol = (aw + pw) // dmod
    assert (aw + pw) % dmod == 0
    npt = dm.tp // tm
    row = lambda i: (i, 0)
    prow = lambda i: (jnp.minimum(i, npt - 1), 0)
    srow = lambda i: (jnp.maximum(i - npt, 0), 0)
    return pl.pallas_call(
        functools.partial(_post_body, alpha=dm.alpha, n_prompt_tiles=npt),
        name="post_mixer",
        grid=(dm.t // tm,),
        in_specs=[
            pl.BlockSpec((tm, aw), prow),
            pl.BlockSpec((tm, aw), srow),
            pl.BlockSpec((tm, pw), prow),
            pl.BlockSpec((tm, pw), srow),
            pl.BlockSpec((tm, dmod), lambda i: (i, gcol)),
            pl.BlockSpec((tm, dmod), lambda i: (i, gcol + 1)),
            pl.BlockSpec((tm, dmod), row),
            _const_spec(wpool.shape), _const_spec(pscale.shape), _const_spec(wba.shape),
            _const_spec(wbb.shape), _const_spec(wout.shape), _const_spec(g1.shape), _const_spec(b1.shape),
            _const_spec(wr.shape),
        ],
        out_specs=[
            pl.BlockSpec((tm, dmod), row),
            pl.BlockSpec((tm * dm.row_lines, LANES), row),
            pl.BlockSpec((tm, ROUTER_LANES), row),
        ],
        out_shape=[
            jax.ShapeDtypeStruct((dm.t, dmod), F32),
            jax.ShapeDtypeStruct((dm.t * dm.row_lines, LANES), U32),
            jax.ShapeDtypeStruct((dm.t, ROUTER_LANES), F32),
        ],
        compiler_params=_cparams(("arbitrary",)),
    )(a_p, a_s, d_p, d_s, z, z, xf, wpool, pscale, wba, wbb, wout, g1, b1, wr)


def _first_index_of_max(x, iota, n):
    mx = jnp.max(x, axis=0, keepdims=True)
    return jnp.min(jnp.where(x == mx, iota, n), axis=0, keepdims=True)


def _route_body(lg_ref, bias_ref, tri_ref, eidx_ref, gw_ref, rank_ref, cnt_ref, carry, *, n_experts):
    i = pl.program_id(0)

    @pl.when(i == 0)
    def _():
        carry[...] = jnp.zeros_like(carry)

    tr = lg_ref.shape[0]
    gsz = n_experts // N_EXPERT_GROUPS
    neg = -jnp.inf
    s = jax.nn.sigmoid(lg_ref[...].T[0:n_experts, :])
    sb = s + bias_ref[...]
    io_g = lax.broadcasted_iota(I32, (gsz, tr), 0).astype(F32)
    gs_rows = []
    for gi in range(N_EXPERT_GROUPS):
        blk = sb[gi * gsz:(gi + 1) * gsz, :]
        m1 = jnp.max(blk, axis=0, keepdims=True)
        first = _first_index_of_max(blk, io_g, gsz)
        m2 = jnp.max(jnp.where(io_g == first, neg, blk), axis=0, keepdims=True)
        gs_rows.append(m1 + m2)
    gs = jnp.concatenate(gs_rows, axis=0)
    io_ng = lax.broadcasted_iota(I32, (N_EXPERT_GROUPS, tr), 0).astype(F32)
    gsel = jnp.zeros((N_EXPERT_GROUPS, tr), F32)
    cur = gs
    for _ in range(TOPK_GROUPS):
        hit = io_ng == _first_index_of_max(cur, io_ng, N_EXPERT_GROUPS)
        gsel = jnp.where(hit, 1.0, gsel)
        cur = jnp.where(hit, neg, cur)
    emask = jnp.concatenate(
        [jnp.broadcast_to(gsel[gi:gi + 1, :], (gsz, tr)) for gi in range(N_EXPERT_GROUPS)], axis=0)
    cand = jnp.where(emask > 0.0, sb, neg)
    io_e = lax.broadcasted_iota(I32, (n_experts, tr), 0).astype(F32)
    sel = jnp.zeros((n_experts, tr), F32)
    idx_rows, w_rows = [], []
    for _ in range(TOP_K):
        idx = _first_index_of_max(cand, io_e, n_experts)
        hit = io_e == idx
        w_rows.append(jnp.sum(jnp.where(hit, s, 0.0), axis=0, keepdims=True))
        idx_rows.append(idx)
        sel = jnp.where(hit, 1.0, sel)
        cand = jnp.where(hit, neg, cand)
    w = jnp.concatenate(w_rows, axis=0)
    gw_ref[...] = w / jnp.sum(w, axis=0, keepdims=True) * ROUTED_SCALE
    eidx_ref[...] = jnp.concatenate(idx_rows, axis=0).astype(I32)
    pos = carry[...] + jnp.dot(sel.astype(BF16), tri_ref[...], preferred_element_type=F32)
    rank_rows = [jnp.sum(jnp.where(io_e == idx_rows[k], pos, 0.0), axis=0, keepdims=True) for k in range(TOP_K)]
    rank_ref[...] = jnp.concatenate(rank_rows, axis=0).astype(I32)
    total = carry[...] + jnp.sum(sel, axis=1, keepdims=True)
    carry[...] = total
    cnt_ref[...] = total.astype(I32)


def _route(dm, lg, bias_col, tri):
    tr, e = dm.tr_route, dm.n_experts
    blk = lambda i: (0, i)
    return pl.pallas_call(
        functools.partial(_route_body, n_experts=e),
        name="route",
        grid=(dm.t // tr,),
        in_specs=[pl.BlockSpec((tr, ROUTER_LANES), lambda i: (i, 0)), _const_spec((e, 1)), _const_spec((tr, tr))],
        out_specs=[pl.BlockSpec((TOP_K, tr), blk), pl.BlockSpec((TOP_K, tr), blk), pl.BlockSpec((TOP_K, tr), blk),
                   pl.BlockSpec((e, 1), lambda i: (0, 0))],
        out_shape=[jax.ShapeDtypeStruct((TOP_K, dm.t), I32), jax.ShapeDtypeStruct((TOP_K, dm.t), F32),
                   jax.ShapeDtypeStruct((TOP_K, dm.t), I32), jax.ShapeDtypeStruct((e, 1), I32)],
        scratch_shapes=[pltpu.VMEM((e, 1), F32)],
        compiler_params=_cparams(("arbitrary",)),
    )(lg, bias_col, tri)


def _dest_body(pstart_ref, eidx_ref, rank_ref, dest_ref, *, n_experts):
    eidx = eidx_ref[...]

    def add(e, acc):
        return acc + jnp.where(eidx == e, pstart_ref[e], 0)

    dest_ref[...] = lax.fori_loop(0, n_experts, add, rank_ref[...])


def _dest(dm, pstart, eidx, rank):
    tr = dm.tr_dest
    blk = lambda i, *_: (0, i)
    return pl.pallas_call(
        functools.partial(_dest_body, n_experts=dm.n_experts),
        name="dest_rows",
        grid_spec=pltpu.PrefetchScalarGridSpec(
            num_scalar_prefetch=1,
            grid=(dm.t // tr,),
            in_specs=[pl.BlockSpec((TOP_K, tr), blk), pl.BlockSpec((TOP_K, tr), blk)],
            out_specs=pl.BlockSpec((TOP_K, tr), blk),
        ),
        out_shape=jax.ShapeDtypeStruct((TOP_K, dm.t), I32),
        compiler_params=_cparams(("arbitrary",)),
    )(pstart, eidx, rank)


def _dispatch_body(pend_ref, padded_ref, dest_ref, x_ref, xs_ref, zbuf, zsem, sem, *, bm, rl, n_experts):
    i = pl.program_id(0)
    tm = x_ref.shape[0] // rl
    n_rows = xs_ref.shape[0] // rl

    def zero_block(row_start):
        start = pl.multiple_of(row_start * rl, bm * rl)
        return pltpu.make_async_copy(zbuf, xs_ref.at[pl.ds(start, bm * rl)], zsem)

    @pl.when(i == 0)
    def _():
        zbuf[...] = jnp.zeros_like(zbuf)
        used_rows = pend_ref[n_experts - 1]

        def expert_tail(e, fn):
            @pl.when(padded_ref[e] > 0)
            def _():
                fn(zero_block(pend_ref[e] - bm))

        def unused_block(b, fn):
            fn(zero_block(used_rows + b * bm))

        n_unused = (n_rows - used_rows) // bm
        for fn in (lambda c: c.start(), lambda c: c.wait()):
            lax.fori_loop(0, n_experts, lambda e, carry, fn=fn: (expert_tail(e, fn), carry)[1], 0)
            lax.fori_loop(0, n_unused, lambda b, carry, fn=fn: (unused_block(b, fn), carry)[1], 0)

    def start_tok(t, carry):
        src = x_ref.at[pl.ds(pl.multiple_of(t * rl, rl), rl)]
        for k in range(TOP_K):
            dst = pl.multiple_of(dest_ref[0, 0, t * TOP_K + k] * rl, rl)
            pltpu.make_async_copy(src, xs_ref.at[pl.ds(dst, rl)], sem).start(
                priority=k % 2)
        return carry

    lax.fori_loop(0, tm, start_tok, 0)
    n = tm * TOP_K * rl
    pltpu.make_async_copy(xs_ref.at[pl.ds(0, n)], xs_ref.at[pl.ds(0, n)], sem).wait()


def _dispatch(dm, pend, padded, dest_tiles, x1p):
    tm, bm, rl = dm.tm_disp, dm.bm, dm.row_lines
    n_rows = dm.n_blocks * bm
    dest_tiles = dest_tiles.reshape(dm.t // tm, 1, tm * TOP_K)
    return pl.pallas_call(
        functools.partial(_dispatch_body, bm=bm, rl=rl, n_experts=dm.n_experts),
        name="dispatch",
        grid_spec=pltpu.PrefetchScalarGridSpec(
            num_scalar_prefetch=2,
            grid=(dm.t // tm,),
            in_specs=[
                pl.BlockSpec((1, 1, tm * TOP_K), lambda i, *_: (i, 0, 0), memory_space=pltpu.SMEM),
                pl.BlockSpec((tm * rl, LANES), lambda i, *_: (i, 0)),
            ],
            out_specs=pl.BlockSpec(memory_space=pl.ANY),
            scratch_shapes=[pltpu.VMEM((bm * rl, LANES), U32), pltpu.SemaphoreType.DMA, pltpu.SemaphoreType.DMA],
        ),
        out_shape=jax.ShapeDtypeStruct((n_rows * rl, LANES), U32),
        compiler_params=pltpu.CompilerParams(dimension_semantics=("arbitrary",), has_side_effects=True,
                                             vmem_limit_bytes=V7X_VMEM_LIMIT_BYTES),
    )(pend, padded, dest_tiles, x1p)


def _experts_body(be_ref, nu_ref, first_ref, slot_ref, nxt_ref, xs_ref, wg_hbm, wu_hbm, wd_hbm, ys_ref,
                  wg_f, wu_f, wd_f, wg_b, wu_b, wd_b, wsem, *, layer):
    b = pl.program_id(0)

    def fetch(e, slot):
        return [pltpu.make_async_copy(src.at[layer, e], dst.at[slot], wsem.at[slot])
                for src, dst in ((wg_hbm, wg_f), (wu_hbm, wu_f), (wd_hbm, wd_f))]

    @pl.when(b < nu_ref[0])
    def _():
        @pl.when(first_ref[b] == 1)
        def _():
            slot = slot_ref[b]

            @pl.when(b == 0)
            def _():
                for c in fetch(be_ref[0], slot):
                    c.start()

            @pl.when(nxt_ref[b] >= 0)
            def _():
                for c in fetch(nxt_ref[b], 1 - slot):
                    c.start()

            for c in fetch(be_ref[b], slot):
                c.wait()
            wg_b[...] = wg_f[slot].astype(BF16)
            wu_b[...] = wu_f[slot].astype(BF16)
            wd_b[...] = wd_f[slot].astype(BF16)

        rl = wg_b.shape[0] // 2 // LANES
        bm = xs_ref.shape[0] // rl
        xs = jnp.concatenate([_load_row_tile_chunk(xs_ref, j, bm, rl) for j in range(rl)], axis=1)
        lo, hi = _unpack_bf16_pair(xs)
        lo, hi = lo.astype(BF16), hi.astype(BF16)
        half = lo.shape[1]

        def proj(w_ref):
            return (jnp.dot(lo, w_ref[:half, :], preferred_element_type=F32)
                    + jnp.dot(hi, w_ref[half:, :], preferred_element_type=F32))

        act = jax.nn.silu(proj(wg_b)) * proj(wu_b)
        y = jnp.dot(act.astype(BF16), wd_b[...], preferred_element_type=F32)
        _store_row_tiles(ys_ref, _pack_bf16_pair(y[:, :half], y[:, half:]))

    @pl.when(b >= nu_ref[0])
    def _():
        ys_ref[...] = jnp.zeros_like(ys_ref)


def _experts(dm, layer, runs, xs, wg, wu, wd):
    bm, d, ed, rl = dm.bm, dm.d_model, dm.expert_dim, dm.row_lines
    any_spec = pl.BlockSpec(memory_space=pl.ANY)

    return pl.pallas_call(
        functools.partial(_experts_body, layer=layer),
        name="experts",
        grid_spec=pltpu.PrefetchScalarGridSpec(
            num_scalar_prefetch=5,
            grid=(dm.n_blocks,),
            in_specs=[
                pl.BlockSpec((bm * rl, LANES), lambda b, be, nu, *_: (jnp.minimum(b, nu[0] - 1), 0)),
                any_spec, any_spec, any_spec,
            ],
            out_specs=pl.BlockSpec((bm * rl, LANES), lambda b, *_: (b, 0)),
            scratch_shapes=[
                pltpu.VMEM((2, d, ed), F32), pltpu.VMEM((2, d, ed), F32), pltpu.VMEM((2, ed, d), F32),
                pltpu.VMEM((d, ed), BF16), pltpu.VMEM((d, ed), BF16), pltpu.VMEM((ed, d), BF16),
                pltpu.SemaphoreType.DMA((2,)),
            ],
        ),
        out_shape=jax.ShapeDtypeStruct((dm.n_blocks * bm * rl, LANES), U32),
        compiler_params=_cparams(("arbitrary",)),
    )(*runs, xs, wg, wu, wd)


def _combine_body(dest_ref, x1_ref, gw_ref, ys_ref, wgs_ref, wus_ref, wds_ref, g_ref, b_ref,
                  xf_ref, xb_ref, gbuf, sem, *, alpha, n_prompt_tiles):
    tm = x1_ref.shape[0]
    rl = gbuf.shape[1] // tm

    def start_tok(t, carry):
        land = pl.multiple_of(t * rl, rl)
        for k in range(TOP_K):
            src = pl.multiple_of(dest_ref[0, 0, t * TOP_K + k] * rl, rl)
            pltpu.make_async_copy(ys_ref.at[pl.ds(src, rl)], gbuf.at[k, pl.ds(land, rl)], sem).start(
                priority=k % 2)
        return carry

    lax.fori_loop(0, tm, start_tok, 0)
    x1 = x1_ref[...]
    xb = x1.astype(BF16)
    act = (jax.nn.silu(jnp.dot(xb, wgs_ref[...], preferred_element_type=F32))
           * jnp.dot(xb, wus_ref[...], preferred_element_type=F32))
    shared = jnp.dot(act.astype(BF16), wds_ref[...], preferred_element_type=F32)
    pltpu.make_async_copy(gbuf, gbuf, sem).wait()
    gw = gw_ref[...]
    wks = [jnp.broadcast_to(gw[:, k:k + 1], (tm, LANES)) for k in range(TOP_K)]
    r_lo, r_hi = [], []
    for j in range(rl):
        a_lo = jnp.zeros((tm, LANES), F32)
        a_hi = jnp.zeros((tm, LANES), F32)
        for k in range(TOP_K):
            lo, hi = _unpack_bf16_pair(_load_row_tile_chunk(gbuf.at[k], j, tm, rl))
            a_lo = a_lo + lo * wks[k]
            a_hi = a_hi + hi * wks[k]
        r_lo.append(a_lo)
        r_hi.append(a_hi)
    f = shared + jnp.concatenate(r_lo + r_hi, axis=1)
    x2 = _layer_norm(alpha * x1 + f, g_ref[...], b_ref[...])
    if n_prompt_tiles is None:
        xf_ref[...] = x2
        xb_ref[...] = x2.astype(BF16)
    else:
        is_prompt = pl.program_id(0) < n_prompt_tiles

        @pl.when(is_prompt)
        def _():
            xf_ref[...] = x2

        @pl.when(jnp.logical_not(is_prompt))
        def _():
            xb_ref[...] = x2


def _combine(dm, dest_tiles, x1f, gw_tok, ys, wgs, wus, wds, g2, b2, last):
    tm, d, rl = dm.tm_moe, dm.d_model, dm.row_lines
    row = lambda i, *_: (i, 0)
    npt = dm.tp // tm
    if last:
        out_specs = [pl.BlockSpec((tm, d), lambda i, *_: (jnp.minimum(i, npt - 1), 0)),
                     pl.BlockSpec((tm, d), lambda i, *_: (jnp.maximum(i - npt, 0), 0))]
        out_shape = [jax.ShapeDtypeStruct((dm.tp, d), F32), jax.ShapeDtypeStruct((dm.ts, d), F32)]
    else:
        out_specs = [pl.BlockSpec((tm, d), row), pl.BlockSpec((tm, d), row)]
        out_shape = [jax.ShapeDtypeStruct((dm.t, d), F32), jax.ShapeDtypeStruct((dm.t, d), BF16)]

    def const(shape):
        nd = len(shape)
        return pl.BlockSpec(shape, lambda i, *_: (0,) * nd, pipeline_mode=pl.Buffered(1))

    return pl.pallas_call(
        functools.partial(_combine_body, alpha=dm.alpha, n_prompt_tiles=npt if last else None),
        name="combine",
        grid_spec=pltpu.PrefetchScalarGridSpec(
            num_scalar_prefetch=0,
            grid=(dm.t // tm,),
            in_specs=[
                pl.BlockSpec((1, 1, tm * TOP_K), lambda i: (i, 0, 0), memory_space=pltpu.SMEM),
                pl.BlockSpec((tm, d), row),
                pl.BlockSpec((tm, TOP_K), row),
                pl.BlockSpec(memory_space=pl.ANY),
                const(wgs.shape), const(wus.shape), const(wds.shape), const(g2.shape), const(b2.shape),
            ],
            out_specs=out_specs,
            scratch_shapes=[pltpu.VMEM((TOP_K, tm * rl, LANES), U32), pltpu.SemaphoreType.DMA],
        ),
        out_shape=out_shape,
        compiler_params=_cparams(("arbitrary",)),
    )(dest_tiles, x1f, gw_tok, ys, wgs, wus, wds, g2, b2)


def _group_major(w, axis):
    shape = w.shape
    w = w.reshape(shape[:axis] + (N_KV_HEADS, GQA_GROUP, HEAD_DIM) + shape[axis + 1:])
    return jnp.swapaxes(w, axis, axis + 1).reshape(shape)


def _arrange_w_in(dm, w):
    aw, kvw, pw, d = dm.attn_width, dm.kv_width, dm.pool_width, dm.d_model
    k0, v0, u0 = aw, aw + kvw, aw + 2 * kvw
    return jnp.concatenate([_group_major(w[:, :aw], 1), w[:, u0:], w[:, k0:v0], w[:, v0:u0]], axis=1).astype(BF16)


def _moe_plan(dm, eidx, rank, counts):
    bm, e = dm.bm, dm.n_experts
    counts = counts.reshape(e)
    padded = (counts + bm - 1) // bm * bm
    pend = jnp.cumsum(padded)
    pstart = pend - padded
    dest = _dest(dm, pstart.astype(I32), eidx, rank)
    dest_tiles = dest.T.reshape(dm.t // dm.tm_moe, 1, dm.tm_moe * TOP_K)
    block_start = jnp.arange(dm.n_blocks, dtype=I32) * bm
    block_e = jnp.minimum(jnp.sum(block_start[:, None] >= pend[None, :], axis=1), e - 1).astype(I32)
    n_used = (pend[-1] // bm).astype(I32).reshape(1)
    blocks = jnp.arange(dm.n_blocks, dtype=I32)
    prev_e = jnp.concatenate([jnp.full((1,), -1, I32), block_e[:-1]])
    first = jnp.logical_and(blocks < n_used[0], block_e != prev_e).astype(I32)
    slot = (jnp.cumsum(first) - 1) % 2
    ids = jnp.arange(e, dtype=I32)
    at_or_after = lax.cummin(jnp.where(padded > 0, ids, e)[::-1])[::-1]
    after = jnp.concatenate([at_or_after[1:], jnp.full((1,), e, I32)])
    nxt = jnp.take(jnp.where(after < e, after, -1), block_e)
    runs = (block_e, n_used, first, slot.astype(I32), nxt.astype(I32))
    return pend.astype(I32), padded.astype(I32), dest_tiles.astype(I32), runs


def _forward(dm, x_prompt, x_sample, cache_k, cache_v, state_pool, ln_in_g, ln_in_b, w_in, sinks, w_pool,
             pool_scale, w_branch_a, w_branch_b, w_out, ln1_g, ln1_b, w_router, router_bias, w_gate_e, w_up_e,
             w_down_e, w_gate_s, w_up_s, w_down_s, ln2_g, ln2_b):
    d, aw, kvw, pw, e = dm.d_model, dm.attn_width, dm.kv_width, dm.pool_width, dm.n_experts
    row2 = lambda v: v.reshape(1, -1)
    tri = (jnp.arange(dm.tr_route)[:, None] < jnp.arange(dm.tr_route)[None, :]).astype(BF16)
    sink_heads = np.array([GQA_GROUP * h + g for g in range(GQA_GROUP) for h in range(N_KV_HEADS)])
    kcol0 = aw + pw + 2 * d

    xf, xb = _ln_in(dm, x_prompt.reshape(dm.tp, d), x_sample.reshape(dm.ts, d), row2(ln_in_g), row2(ln_in_b))
    outs = [[] for _ in range(6)]
    for l in range(dm.depth):
        z = _in_proj(dm, xb, _arrange_w_in(dm, w_in[l]))
        sink_col = jnp.broadcast_to(jnp.repeat(sinks[l][sink_heads], CHUNK).reshape(aw, 1), (aw, LANES))
        a_p = _attn_prompt(dm, z, sink_col)
        a_s = _attn_sample(dm, z, cache_k[l].reshape(dm.dec_batch * dm.window, kvw),
                           cache_v[l].reshape(dm.dec_batch * dm.window, kvw), sink_col)
        hist = jnp.pad(state_pool[l], ((0, 0), (1, 0), (0, 0))).reshape(dm.dec_batch * POOL_HIST_ROWS, pw)
        d_p = _pool_prompt(dm, z)
        d_s = _pool_sample(dm, z, hist)
        wr = jnp.pad(w_router[l], ((0, 0), (0, ROUTER_LANES - e))).astype(BF16)
        x1f, x1p, lg = _post(dm, a_p, a_s, d_p, d_s, z, xf, w_pool[l].astype(BF16), row2(pool_scale[l]),
                              _group_major(w_branch_a[l], 0).astype(BF16), w_branch_b[l].astype(BF16),
                              w_out[l].astype(BF16), row2(ln1_g[l]), row2(ln1_b[l]), wr)
        eidx, gwt, rank, counts = _route(dm, lg, router_bias[l].reshape(e, 1), tri)
        pend, padded, dest_tiles, runs = _moe_plan(dm, eidx, rank, counts)
        xs = _dispatch(dm, pend, padded, dest_tiles, x1p)
        ys = _experts(dm, l, runs, xs, w_gate_e, w_up_e, w_down_e)
        xf, xb = _combine(dm, dest_tiles, x1f, gwt.T, ys, w_gate_s[l].astype(BF16), w_up_s[l].astype(BF16),
                          w_down_s[l].astype(BF16), row2(ln2_g[l]), row2(ln2_b[l]), last=l == dm.depth - 1)
        def prompt_tail(col0, width, n_rows):
            if dm.t % dm.seq == 0:
                z_seq = z.reshape(dm.t // dm.seq, dm.seq, dm.in_width)
                return z_seq[:dm.batch, dm.seq - n_rows:, col0:col0 + width]
            return jnp.stack([z[(b + 1) * dm.seq - n_rows:(b + 1) * dm.seq, col0:col0 + width]
                              for b in range(dm.batch)])

        def sample_cols(col0, width):
            return z[dm.tp:, col0:col0 + width].reshape(dm.dec_batch, dm.dec_seq, width)

        outs[0].append(prompt_tail(kcol0, kvw, dm.window))
        outs[1].append(prompt_tail(kcol0 + kvw, kvw, dm.window))
        outs[2].append(prompt_tail(aw, pw, 15))
        outs[3].append(sample_cols(kcol0, kvw))
        outs[4].append(sample_cols(kcol0 + kvw, kvw))
        outs[5].append(sample_cols(aw, pw))
    y_prompt = xf.reshape(dm.batch, dm.seq, d)
    y_sample = xb.reshape(dm.dec_batch, dm.dec_seq, d)
    kp, vp, pp, ks, vs, ps = (jnp.stack(o) for o in outs)
    heads = (N_KV_HEADS, HEAD_DIM)
    new_k_prompt = kp.reshape(dm.depth, dm.batch, dm.window, *heads)
    new_v_prompt = vp.reshape(dm.depth, dm.batch, dm.window, *heads)
    ks = ks.reshape(dm.depth, dm.dec_batch, dm.dec_seq, *heads)
    vs = vs.reshape(dm.depth, dm.dec_batch, dm.dec_seq, *heads)
    new_k_sample = jnp.concatenate([cache_k, ks], axis=2)[:, :, -dm.window:]
    new_v_sample = jnp.concatenate([cache_v, vs], axis=2)[:, :, -dm.window:]
    new_pool_sample = jnp.concatenate([state_pool, ps], axis=2)[:, :, -15:]
    return (y_prompt, y_sample, new_k_prompt, new_v_prompt, pp, new_k_sample, new_v_sample, new_pool_sample)


def kernel(x_prompt, x_sample, cache_k, cache_v, state_pool, ln_in_g, ln_in_b, w_in, sinks, w_pool, pool_scale,
           w_branch_a, w_branch_b, w_out, ln1_g, ln1_b, w_router, router_bias, w_gate_e, w_up_e, w_down_e,
           w_gate_s, w_up_s, w_down_s, ln2_g, ln2_b):
    dm = Dims()
    return _forward(dm, x_prompt, x_sample, cache_k, cache_v, state_pool, ln_in_g, ln_in_b, w_in, sinks, w_pool,
                    pool_scale, w_branch_a, w_branch_b, w_out, ln1_g, ln1_b, w_router, router_bias, w_gate_e,
                    w_up_e, w_down_e, w_gate_s, w_up_s, w_down_s, ln2_g, ln2_b)
```
